```python
import jax, jax.numpy as jnp
from jax import lax
import numpy as np

D_MODEL = 1024
BATCH = 8
SEQ = 4096
DEPTH = 4

N_A = DEPTH // 2
N_B = DEPTH - N_A
HEAD_DIM = 64
MIX_WIDTH = D_MODEL
MEM_LEN = 256
MEM_HEADS = 4
MEM_WIDTH = MEM_HEADS * HEAD_DIM
MAIN_WIDTH = MIX_WIDTH - MEM_WIDTH
POOL_WINDOWS = (2, 4, 8, 16)
POOL_GROUPS = len(POOL_WINDOWS)
POOL_GROUP_DIM = MAIN_WIDTH // POOL_GROUPS
SWA_Q_HEADS = MAIN_WIDTH // HEAD_DIM
SWA_KV_HEADS = 4
SWA_GROUP = SWA_Q_HEADS // SWA_KV_HEADS
KV_WIDTH = 2 * SWA_KV_HEADS * HEAD_DIM
MEM_KV_WIDTH = 2 * MEM_WIDTH
WINDOW = 128
BLOCK = 128
D_FF = 4 * D_MODEL
EPS = 1e-6

kernel_name = "yoco_pool_swa_sink_hybrid"


def rmsnorm(x, g):
    xf = x.astype(jnp.float32)
    y = xf * lax.rsqrt(jnp.mean(xf * xf, axis=-1, keepdims=True) + EPS)
    return (y * g.astype(jnp.float32)).astype(x.dtype)


def alibi_slopes(n):
    return jnp.exp2(-8.0 * jnp.arange(1, n + 1, dtype=jnp.float32) / n)


def pool_mixer(u, pool_w, pool_scale):
    B, S, _ = u.shape
    uf = u.astype(jnp.float32).reshape(B, S, POOL_GROUPS, POOL_GROUP_DIM)
    csum = jnp.concatenate([jnp.zeros((B, 1, POOL_GROUPS, POOL_GROUP_DIM), jnp.float32),
                            jnp.cumsum(uf, axis=1)], axis=1)
    win = jnp.array(POOL_WINDOWS, jnp.int32)
    t = jnp.arange(S, dtype=jnp.int32)[:, None]
    lo = jnp.maximum(t + 1 - win[None, :], 0)
    cnt = jnp.minimum(t + 1, win[None, :]).astype(jnp.float32)
    window_sum = csum[:, 1:] - csum[:, lo, jnp.arange(POOL_GROUPS)[None, :]]
    d = (window_sum / cnt[None, :, :, None] - uf).astype(u.dtype)
    mixed = jnp.einsum('bsgc,gcd->bsgd', d, pool_w)
    return mixed.reshape(B, S, MAIN_WIDTH) * pool_scale


def swa_sink_attention(q, k, v, sinks):
    B, S = q.shape[0], q.shape[1]
    nb = S // BLOCK
    qb = q.reshape(B, nb, BLOCK, SWA_KV_HEADS, SWA_GROUP, HEAD_DIM)

    def with_prev(a):
        ab = a.reshape(B, nb, BLOCK, SWA_KV_HEADS, HEAD_DIM)
        prev = jnp.pad(ab[:, :-1], ((0, 0), (1, 0), (0, 0), (0, 0), (0, 0)))
        return jnp.concatenate([prev, ab], axis=2)

    kb, vb = with_prev(k), with_prev(v)
    s = jnp.einsum('bnqkgd,bnpkd->bnkgqp', qb, kb).astype(jnp.float32) * (HEAD_DIM ** -0.5)
    blk = jnp.arange(nb, dtype=jnp.int32)[:, None] * BLOCK
    qpos = blk + jnp.arange(BLOCK, dtype=jnp.int32)[None, :]
    kpos = blk - BLOCK + jnp.arange(2 * BLOCK, dtype=jnp.int32)[None, :]
    dist = qpos[:, :, None] - kpos[:, None, :]
    valid = (dist >= 0) & (dist < WINDOW) & (kpos[:, None, :] >= 0)
    slopes = alibi_slopes(SWA_Q_HEADS).reshape(SWA_KV_HEADS, SWA_GROUP)
    s = s - slopes[None, None, :, :, None, None] * dist.astype(jnp.float32)[None, :, None, None]
    s = jnp.where(valid[None, :, None, None], s, jnp.finfo(jnp.float32).min)
    sink = sinks.astype(jnp.float32).reshape(SWA_KV_HEADS, SWA_GROUP)[None, None, :, :, None, None]
    m = jnp.maximum(jnp.max(s, axis=-1, keepdims=True), sink)
    e = jnp.exp(s - m)
    p = e / (jnp.sum(e, axis=-1, keepdims=True) + jnp.exp(sink - m))
    o = jnp.einsum('bnkgqp,bnpkd->bnqkgd', p.astype(vb.dtype), vb)
    return o.reshape(B, S, SWA_Q_HEADS * HEAD_DIM)


def memory_attention(q, mk, mv):
    s = jnp.einsum('bshd,bmhd->bhsm', q, mk).astype(jnp.float32) * (HEAD_DIM ** -0.5)
    p = jax.nn.softmax(s, axis=-1)
    o = jnp.einsum('bhsm,bmhd->bshd', p.astype(mv.dtype), mv)
    return o.reshape(q.shape[0], q.shape[1], MEM_WIDTH)


def sq_relu_mlp(x, w_up, w_down):
    h = jax.nn.relu(x @ w_up)
    return (h * h) @ w_down


def setup_inputs(seed: int = 0) -> dict:
    key = jax.random.key(seed)
    ks = jax.random.split(key, 20)
    f32 = jnp.float32

    def nrm(k, shape, scale):
        return jax.random.normal(k, shape, f32) * scale

    def gain(k, shape):
        return 1.0 + 0.02 * jax.random.normal(k, shape, f32)

    return {
        "x": nrm(ks[0], (BATCH, SEQ, D_MODEL), 1.0),
        "mem": nrm(ks[1], (BATCH, MEM_LEN, D_MODEL), 1.0),
        "norm_mix": gain(ks[2], (DEPTH, D_MODEL)),
        "w_in": nrm(ks[3], (DEPTH, D_MODEL, MIX_WIDTH), D_MODEL ** -0.5),
        "pool_w": nrm(ks[4], (N_A, POOL_GROUPS, POOL_GROUP_DIM, POOL_GROUP_DIM), POOL_GROUP_DIM ** -0.5),
        "pool_scale": gain(ks[5], (N_A, MAIN_WIDTH)),
        "kv_norm": gain(ks[6], (D_MODEL,)),
        "w_kv": nrm(ks[7], (D_MODEL, KV_WIDTH), D_MODEL ** -0.5),
        "k_norm": gain(ks[8], (HEAD_DIM,)),
        "q_norm": gain(ks[9], (N_B, HEAD_DIM)),
        "sinks": nrm(ks[10], (N_B, SWA_Q_HEADS), 0.5),
        "mem_norm": gain(ks[11], (DEPTH, D_MODEL)),
        "w_mem_kv": nrm(ks[12], (DEPTH, D_MODEL, MEM_KV_WIDTH), D_MODEL ** -0.5),
        "mem_q_norm": gain(ks[13], (DEPTH, HEAD_DIM)),
        "mem_k_norm": gain(ks[14], (DEPTH, HEAD_DIM)),
        "w_out": nrm(ks[15], (DEPTH, MIX_WIDTH, D_MODEL), MIX_WIDTH ** -0.5),
        "norm_mlp": gain(ks[16], (DEPTH, D_MODEL)),
        "w_up": nrm(ks[17], (DEPTH, D_MODEL, D_FF), D_MODEL ** -0.5),
        "w_down": nrm(ks[18], (DEPTH, D_FF, D_MODEL), D_FF ** -0.5),
    }


def reference(x, mem, norm_mix, w_in, pool_w, pool_scale, kv_norm, w_kv, k_norm, q_norm, sinks,
              mem_norm, w_mem_kv, mem_q_norm, mem_k_norm, w_out, norm_mlp, w_up, w_down):
    B, S, _ = x.shape
    h = x
    k_shared = None
    v_shared = None
    for l in range(DEPTH):
        if l == N_A:
            kv = rmsnorm(h, kv_norm) @ w_kv
            k_shared = rmsnorm(kv[..., :KV_WIDTH // 2].reshape(B, S, SWA_KV_HEADS, HEAD_DIM), k_norm)
            v_shared = kv[..., KV_WIDTH // 2:].reshape(B, S, SWA_KV_HEADS, HEAD_DIM)

        proj = rmsnorm(h, norm_mix[l]) @ w_in[l]
        main, mq = proj[..., :MAIN_WIDTH], proj[..., MAIN_WIDTH:]
        if l < N_A:
            main_out = pool_mixer(main, pool_w[l], pool_scale[l])
        else:
            j = l - N_A
            q = rmsnorm(main.reshape(B, S, SWA_Q_HEADS, HEAD_DIM), q_norm[j])
            main_out = swa_sink_attention(q, k_shared, v_shared, sinks[j])

        mkv = rmsnorm(mem, mem_norm[l]) @ w_mem_kv[l]
        mk = rmsnorm(mkv[..., :MEM_WIDTH].reshape(B, MEM_LEN, MEM_HEADS, HEAD_DIM), mem_k_norm[l])
        mv = mkv[..., MEM_WIDTH:].reshape(B, MEM_LEN, MEM_HEADS, HEAD_DIM)
        mqh = rmsnorm(mq.reshape(B, S, MEM_HEADS, HEAD_DIM), mem_q_norm[l])
        mem_out = memory_attention(mqh, mk, mv)

        h = h + jnp.concatenate([main_out, mem_out], axis=-1) @ w_out[l]
        h = h + sq_relu_mlp(rmsnorm(h, norm_mlp[l]), w_up[l], w_down[l])
    return h
```

```python
import functools

import jax
import jax.numpy as jnp
from jax import lax
from jax.experimental import pallas as pl
from jax.experimental.pallas import tpu as pltpu

F32 = jnp.float32
BF16 = jnp.bfloat16

D_MODEL = 1024
DEPTH = 4
N_A = DEPTH // 2
HEAD_DIM = 64
MEM_LEN = 256
MEM_HEADS = 4
MEM_WIDTH = MEM_HEADS * HEAD_DIM
MAIN_WIDTH = D_MODEL - MEM_WIDTH
POOL_WINDOWS = (2, 4, 8, 16)
POOL_GROUPS = len(POOL_WINDOWS)
POOL_GROUP_DIM = MAIN_WIDTH // POOL_GROUPS
SWA_Q_HEADS = MAIN_WIDTH // HEAD_DIM
SWA_KV_HEADS = 4
SWA_GROUP = SWA_Q_HEADS // SWA_KV_HEADS
KV_HALF = SWA_KV_HEADS * HEAD_DIM
BLOCK = 128
D_FF = 4 * D_MODEL
EPS = 1e-6
SCALE = HEAD_DIM ** -0.5
NEG = -1e30

GROUP_LANES = 256
N_GROUPS = D_MODEL // GROUP_LANES
POOL_HALO = 16

TM_MIX = 512
TM_MLP = 512
TM_KV = 512
VMEM_LIMIT = 52 * 1024 * 1024


def _dot(a, b):
    return jnp.dot(a, b, preferred_element_type=F32)


def _rms(x, g):
    ms = jnp.mean(x * x, axis=-1, keepdims=True)
    return x * lax.rsqrt(ms + EPS) * g


def _seg_sumsq(x, seg):
    sq = x * x
    hi = sq.astype(BF16)
    lo = (sq - hi.astype(F32)).astype(BF16)
    return _dot(hi, seg) + _dot(lo, seg)


def _head_norm(x, seg, gain):
    ss = _seg_sumsq(x, seg)
    return x * lax.rsqrt(ss * (1.0 / HEAD_DIM) + EPS) * gain


def _softmax_rows(s):
    m = jnp.max(s, axis=-1, keepdims=True)
    e = jnp.exp(s - m)
    return e * (1.0 / jnp.sum(e, axis=-1, keepdims=True))


def _rows_only(a, r0, r1, dst0=None):
    n = a.shape[0]
    dst0 = r0 if dst0 is None else dst0
    pieces = []
    if dst0 > 0:
        pieces.append(jnp.zeros((dst0, a.shape[1]), a.dtype))
    pieces.append(a[r0:r1])
    rest = n - dst0 - (r1 - r0)
    if rest > 0:
        pieces.append(jnp.zeros((rest, a.shape[1]), a.dtype))
    return jnp.concatenate(pieces, axis=0)


def _mem_kv_kernel(mem_ref, g_ref, w_ref, gk_ref, seg_ref, mkt_ref, mvw_ref):
    xn = _rms(mem_ref[0], g_ref[0]).astype(BF16)
    kv = _dot(xn, w_ref[0])
    kn = _head_norm(kv[:, :MEM_WIDTH], seg_ref[...], gk_ref[0])
    mkt_ref[0, 0] = kn.T.astype(BF16)
    mvw_ref[0, 0] = kv[:, MEM_WIDTH:].astype(BF16)


def _mem_kv(mem, mem_norm, w_memkv_pad, gk_mem, seg):
    B = mem.shape[0]
    wide = w_memkv_pad.shape[-1]
    return pl.pallas_call(
        _mem_kv_kernel,
        grid=(DEPTH, B),
        in_specs=[
            pl.BlockSpec((1, MEM_LEN, D_MODEL), lambda l, b: (b, 0, 0)),
            pl.BlockSpec((1, 1, D_MODEL), lambda l, b: (l, 0, 0)),
            pl.BlockSpec((1, D_MODEL, wide), lambda l, b: (l, 0, 0)),
            pl.BlockSpec((1, 1, MEM_WIDTH), lambda l, b: (l, 0, 0)),
            pl.BlockSpec((GROUP_LANES, GROUP_LANES), lambda l, b: (0, 0)),
        ],
        out_specs=[
            pl.BlockSpec((1, 1, MEM_WIDTH, MEM_LEN), lambda l, b: (l, b, 0, 0)),
            pl.BlockSpec((1, 1, MEM_LEN, wide - MEM_WIDTH), lambda l, b: (l, b, 0, 0)),
        ],
        out_shape=[
            jax.ShapeDtypeStruct((DEPTH, B, MEM_WIDTH, MEM_LEN), BF16),
            jax.ShapeDtypeStruct((DEPTH, B, MEM_LEN, wide - MEM_WIDTH), BF16),
        ],
        compiler_params=pltpu.CompilerParams(
            dimension_semantics=("arbitrary", "arbitrary"), vmem_limit_bytes=VMEM_LIMIT),
        name="mem_kv",
    )(mem, mem_norm, w_memkv_pad, gk_mem, seg)


def _kv_kernel(h_ref, g_ref, w_ref, gk_ref, seg_ref, kt_ref, v_ref):
    xn = _rms(h_ref[0], g_ref[...]).astype(BF16)
    kv = _dot(xn, w_ref[...])
    kn = _head_norm(kv[:, :KV_HALF], seg_ref[...], gk_ref[...])
    kt_ref[0] = kn.T.astype(BF16)
    v_ref[0] = kv[:, KV_HALF:].astype(BF16)


def _shared_kv(h, kv_norm, w_kv, gk, seg):
    B, S, _ = h.shape
    return pl.pallas_call(
        _kv_kernel,
        grid=(B, S // TM_KV),
        in_specs=[
            pl.BlockSpec((1, TM_KV, D_MODEL), lambda b, i: (b, i, 0)),
            pl.BlockSpec((1, D_MODEL), lambda b, i: (0, 0)),
            pl.BlockSpec((D_MODEL, 2 * KV_HALF), lambda b, i: (0, 0)),
            pl.BlockSpec((1, KV_HALF), lambda b, i: (0, 0)),
            pl.BlockSpec((GROUP_LANES, GROUP_LANES), lambda b, i: (0, 0)),
        ],
        out_specs=[
            pl.BlockSpec((1, KV_HALF, TM_KV), lambda b, i: (b, 0, i)),
            pl.BlockSpec((1, TM_KV, KV_HALF), lambda b, i: (b, i, 0)),
        ],
        out_shape=[
            jax.ShapeDtypeStruct((B, KV_HALF, S), BF16),
            jax.ShapeDtypeStruct((B, S, KV_HALF), BF16),
        ],
        compiler_params=pltpu.CompilerParams(
            dimension_semantics=("arbitrary", "arbitrary"), vmem_limit_bytes=VMEM_LIMIT),
        name="shared_kv",
    )(h, kv_norm, w_kv, gk, seg)


def _mix_a_kernel(h_ref, g_ref, win_ref, gq_ref, seg_ref, wl_ref, invw_ref, poolw_ref, pscale_ref,
                  mkt_ref, mvw_ref, wout_ref, o_ref, halo_ref):
    i = pl.program_id(1)
    tm = h_ref.shape[1]

    @pl.when(i == 0)
    def _():
        halo_ref[...] = jnp.zeros_like(halo_ref)

    x = h_ref[0]
    xn = _rms(x, g_ref[...]).astype(BF16)
    proj = _dot(xn, win_ref[...])

    ext = jnp.concatenate([halo_ref[...], proj], axis=0)
    halo_ref[...] = proj[tm - POOL_HALO:]
    s1 = ext + pltpu.roll(ext, 1, 0)
    s2 = s1 + pltpu.roll(s1, 2, 0)
    s3 = s2 + pltpu.roll(s2, 4, 0)
    s4 = s3 + pltpu.roll(s3, 8, 0)
    wl = wl_ref[...]
    ssel = jnp.where(wl == 2.0, s1, jnp.where(wl == 4.0, s2, jnp.where(wl == 8.0, s3, s4)))
    ssel = ssel[POOL_HALO:]
    t1 = (lax.broadcasted_iota(jnp.int32, (POOL_HALO, D_MODEL), 0) + (i * tm + 1)).astype(F32)
    cnt = jnp.minimum(t1, jnp.maximum(wl, 1.0))
    pooled = jnp.concatenate([ssel[:POOL_HALO] / cnt, ssel[POOL_HALO:] * invw_ref[...]], axis=0)
    d = jnp.where(wl > 0.0, pooled - proj, 0.0).astype(BF16)

    mixed = []
    memo = []
    mkt = mkt_ref[0, 0]
    for g in range(N_GROUPS):
        lanes = slice(g * GROUP_LANES, (g + 1) * GROUP_LANES)
        mixed.append(_dot(d[:, lanes], poolw_ref[g]) * pscale_ref[:, lanes])
        qn = _head_norm(proj[:, lanes], seg_ref[...], gq_ref[:, lanes]).astype(BF16)
        ktm = _rows_only(mkt, g * HEAD_DIM, (g + 1) * HEAD_DIM, dst0=GROUP_LANES - HEAD_DIM)
        p = _softmax_rows(_dot(qn, ktm) * SCALE).astype(BF16)
        memo.append(_dot(p, mvw_ref[0, 0, :, lanes]))
    cat = (jnp.concatenate(mixed, axis=1) + jnp.concatenate(memo, axis=1)).astype(BF16)
    o_ref[0] = x + _dot(cat, wout_ref[...])


def _mix_a(layer, h, g, w_in, gq, seg, wl, invw, poolw, pscale, mkt, mvw, w_out):
    B, S, _ = h.shape
    tm = TM_MIX
    const2 = lambda b, i: (0, 0)
    return pl.pallas_call(
        _mix_a_kernel,
        grid=(B, S // tm),
        in_specs=[
            pl.BlockSpec((1, tm, D_MODEL), lambda b, i: (b, i, 0)),
            pl.BlockSpec((1, D_MODEL), const2),
            pl.BlockSpec((D_MODEL, D_MODEL), const2),
            pl.BlockSpec((1, D_MODEL), const2),
            pl.BlockSpec((GROUP_LANES, GROUP_LANES), const2),
            pl.BlockSpec((1, D_MODEL), const2),
            pl.BlockSpec((1, D_MODEL), const2),
            pl.BlockSpec((N_GROUPS, GROUP_LANES, GROUP_LANES), lambda b, i: (0, 0, 0)),
            pl.BlockSpec((1, D_MODEL), const2),
            pl.BlockSpec((1, 1, MEM_WIDTH, MEM_LEN), lambda b, i: (layer, b, 0, 0)),
            pl.BlockSpec((1, 1, MEM_LEN, D_MODEL), lambda b, i: (layer, b, 0, 0)),
            pl.BlockSpec((D_MODEL, D_MODEL), const2),
        ],
        out_specs=pl.BlockSpec((1, tm, D_MODEL), lambda b, i: (b, i, 0)),
        out_shape=jax.ShapeDtypeStruct(h.shape, F32),
        scratch_shapes=[pltpu.VMEM((POOL_HALO, D_MODEL), F32)],
        compiler_params=pltpu.CompilerParams(
            dimension_semantics=("arbitrary", "arbitrary"), vmem_limit_bytes=VMEM_LIMIT),
        name=f"mix_pool_{layer}",
    )(h, g, w_in, gq, seg, wl, invw, poolw, pscale, mkt, mvw, w_out)


def _mix_b_kernel(sinks_ref, h_ref, g_ref, win_ref, gq_ref, seg_ref, ktc_ref, ktp_ref, vc_ref, vp_ref,
                  bias_ref, mkt_ref, mvw_ref, wout_ref, o_ref, qn_ref, cat_ref):
    i = pl.program_id(1)
    tm = h_ref.shape[1]
    x = h_ref[0]
    xn = _rms(x, g_ref[...]).astype(BF16)
    proj = _dot(xn, win_ref[...])
    for c in range(N_GROUPS):
        lanes = slice(c * GROUP_LANES, (c + 1) * GROUP_LANES)
        qn_ref[:, lanes] = _head_norm(proj[:, lanes], seg_ref[...], gq_ref[:, lanes]).astype(BF16)

    lane_head = lax.broadcasted_iota(jnp.int32, (2 * BLOCK, KV_HALF), 1) // HEAD_DIM
    key_lane = lax.broadcasted_iota(jnp.int32, (1, 2 * BLOCK), 1)
    pen_row = jnp.where(jnp.logical_and(key_lane < BLOCK, i == 0), NEG, 0.0).astype(F32)

    mkt = mkt_ref[0, 0]
    mvst = jnp.concatenate(
        [mvw_ref[0, 0, :, hm * GROUP_LANES:(hm + 1) * GROUP_LANES] for hm in range(MEM_HEADS)], axis=0)
    mem_lanes = slice(SWA_GROUP * GROUP_LANES, D_MODEL)

    for qb in range(tm // BLOCK):
        r0 = qb * BLOCK
        rows = slice(r0, r0 + BLOCK)
        if qb == 0:
            ktw = jnp.concatenate([ktp_ref[0], ktc_ref[0, :, 0:BLOCK]], axis=1)
            vw = jnp.concatenate([vp_ref[0], vc_ref[0, 0:BLOCK]], axis=0)
        else:
            ktw = ktc_ref[0, :, r0 - BLOCK:r0 + BLOCK]
            vw = vc_ref[0, r0 - BLOCK:r0 + BLOCK]
        qst = jnp.concatenate(
            [qn_ref[rows, g * GROUP_LANES:(g + 1) * GROUP_LANES] for g in range(SWA_GROUP)], axis=0)
        p_all = []
        v_all = []
        for kvh in range(SWA_KV_HEADS):
            ktm = _rows_only(ktw, kvh * HEAD_DIM, (kvh + 1) * HEAD_DIM)
            s_all = _dot(qst, ktm)
            ps = []
            for g in range(SWA_GROUP):
                hq = kvh * SWA_GROUP + g
                s = s_all[g * BLOCK:(g + 1) * BLOCK] * SCALE + bias_ref[hq]
                if qb == 0:
                    s = s + pen_row
                sink = sinks_ref[hq]
                m = jnp.maximum(jnp.max(s, axis=-1, keepdims=True), sink)
                e = jnp.exp(s - m)
                den = jnp.sum(e, axis=-1, keepdims=True) + jnp.exp(sink - m)
                ps.append((e * (1.0 / den)).astype(BF16))
            p_all.append(jnp.concatenate(ps, axis=0))
            v_all.append(jnp.where(lane_head == kvh, vw, jnp.zeros_like(vw)))
        o = _dot(jnp.concatenate(p_all, axis=1), jnp.concatenate(v_all, axis=0))
        for g in range(SWA_GROUP):
            cat_ref[rows, g * GROUP_LANES:(g + 1) * GROUP_LANES] = o[g * BLOCK:(g + 1) * BLOCK].astype(BF16)

        qm = qn_ref[rows, mem_lanes]
        pm = []
        for hm in range(MEM_HEADS):
            ktm = _rows_only(mkt, hm * HEAD_DIM, (hm + 1) * HEAD_DIM)
            pm.append(_softmax_rows(_dot(qm, ktm) * SCALE).astype(BF16))
        cat_ref[rows, mem_lanes] = _dot(jnp.concatenate(pm, axis=1), mvst).astype(BF16)

    o_ref[0] = x + _dot(cat_ref[...], wout_ref[...])


def _mix_b(layer, h, sinks, g, w_in, gq, seg, kt, v, bias, mkt, mvw, w_out):
    B, S, _ = h.shape
    tm = TM_MIX
    nq = tm // BLOCK
    const2 = lambda b, i: (0, 0)
    return pl.pallas_call(
        _mix_b_kernel,
        grid=(B, S // tm),
        in_specs=[
            pl.BlockSpec(memory_space=pltpu.SMEM),
            pl.BlockSpec((1, tm, D_MODEL), lambda b, i: (b, i, 0)),
            pl.BlockSpec((1, D_MODEL), const2),
            pl.BlockSpec((D_MODEL, D_MODEL), const2),
            pl.BlockSpec((1, D_MODEL), const2),
            pl.BlockSpec((GROUP_LANES, GROUP_LANES), const2),
            pl.BlockSpec((1, KV_HALF, tm), lambda b, i: (b, 0, i)),
            pl.BlockSpec((1, KV_HALF, BLOCK), lambda b, i: (b, 0, jnp.maximum(i * nq - 1, 0))),
            pl.BlockSpec((1, tm, KV_HALF), lambda b, i: (b, i, 0)),
            pl.BlockSpec((1, BLOCK, KV_HALF), lambda b, i: (b, jnp.maximum(i * nq - 1, 0), 0)),
            pl.BlockSpec((SWA_Q_HEADS, BLOCK, 2 * BLOCK), lambda b, i: (0, 0, 0)),
            pl.BlockSpec((1, 1, MEM_WIDTH, MEM_LEN), lambda b, i: (layer, b, 0, 0)),
            pl.BlockSpec((1, 1, MEM_LEN, D_MODEL), lambda b, i: (layer, b, 0, 0)),
            pl.BlockSpec((D_MODEL, D_MODEL), const2),
        ],
        out_specs=pl.BlockSpec((1, tm, D_MODEL), lambda b, i: (b, i, 0)),
        out_shape=jax.ShapeDtypeStruct(h.shape, F32),
        scratch_shapes=[pltpu.VMEM((tm, D_MODEL), BF16), pltpu.VMEM((tm, D_MODEL), BF16)],
        compiler_params=pltpu.CompilerParams(
            dimension_semantics=("arbitrary", "arbitrary"), vmem_limit_bytes=VMEM_LIMIT),
        name=f"mix_swa_{layer}",
    )(sinks, h, g, w_in, gq, seg, kt, kt, v, v, bias, mkt, mvw, w_out)


def _mlp_kernel(h_ref, g_ref, wup_ref, wdn_ref, o_ref):
    x = h_ref[...]
    xn = _rms(x, g_ref[...]).astype(BF16)
    u = jnp.maximum(_dot(xn, wup_ref[...]), 0.0)
    o_ref[...] = x + _dot((u * u).astype(BF16), wdn_ref[...])


def _mlp(layer, h2d, g, w_up, w_down):
    T = h2d.shape[0]
    tm = TM_MLP
    return pl.pallas_call(
        _mlp_kernel,
        grid=(T // tm,),
        in_specs=[
            pl.BlockSpec((tm, D_MODEL), lambda i: (i, 0)),
            pl.BlockSpec((1, D_MODEL), lambda i: (0, 0)),
            pl.BlockSpec((D_MODEL, D_FF), lambda i: (0, 0), pipeline_mode=pl.Buffered(1)),
            pl.BlockSpec((D_FF, D_MODEL), lambda i: (0, 0), pipeline_mode=pl.Buffered(1)),
        ],
        out_specs=pl.BlockSpec((tm, D_MODEL), lambda i: (i, 0)),
        out_shape=jax.ShapeDtypeStruct(h2d.shape, F32),
        compiler_params=pltpu.CompilerParams(
            dimension_semantics=("arbitrary",), vmem_limit_bytes=VMEM_LIMIT),
        name=f"mlp_{layer}",
    )(h2d, g, w_up, w_down)


def _cols_a(main, memq):
    lead = main.shape[:-1]
    m = main.reshape(*lead, POOL_GROUPS, POOL_GROUP_DIM)
    q = memq.reshape(*lead, MEM_HEADS, HEAD_DIM)
    return jnp.concatenate([m, q], axis=-1).reshape(*lead, D_MODEL)


def _cols_b(main, memq):
    lead = main.shape[:-1]
    m = main.reshape(*lead, SWA_KV_HEADS, SWA_GROUP, HEAD_DIM)
    m = jnp.swapaxes(m, -3, -2).reshape(*lead, MAIN_WIDTH)
    return jnp.concatenate([m, memq], axis=-1)


def _perm_in_out(w_in_l, w_out_l, cols):
    w_in_p = cols(w_in_l[:, :MAIN_WIDTH], w_in_l[:, MAIN_WIDTH:])
    w_out_t = w_out_l.T
    w_out_p = cols(w_out_t[:, :MAIN_WIDTH], w_out_t[:, MAIN_WIDTH:]).T
    return w_in_p.astype(BF16), w_out_p.astype(BF16)


def _block_diag_ones(first_block):
    idx = jnp.arange(GROUP_LANES) // HEAD_DIM
    same = idx[:, None] == idx[None, :]
    keep = idx[:, None] >= first_block
    return jnp.logical_and(same, keep).astype(BF16)


def _swa_bias():
    qi = jnp.arange(BLOCK, dtype=jnp.int32)[:, None]
    kj = jnp.arange(2 * BLOCK, dtype=jnp.int32)[None, :]
    dist = qi + BLOCK - kj
    valid = jnp.logical_and(dist >= 0, dist < BLOCK)
    slopes = jnp.exp2(-8.0 * jnp.arange(1, SWA_Q_HEADS + 1, dtype=F32) / SWA_Q_HEADS)
    return jnp.where(valid[None], -(slopes[:, None, None] * dist.astype(F32)[None]), NEG)


def kernel(x, mem, norm_mix, w_in, pool_w, pool_scale, kv_norm, w_kv, k_norm, q_norm, sinks, mem_norm,
           w_mem_kv, mem_q_norm, mem_k_norm, w_out, norm_mlp, w_up, w_down):
    B, S, _ = x.shape
    zeros_main = jnp.zeros((MAIN_WIDTH,), F32)
    zeros_mem = jnp.zeros((MEM_WIDTH,), F32)
    seg_all = _block_diag_ones(0)
    seg_last = _block_diag_ones(MEM_HEADS - 1)

    wv_blocks = []
    for l in range(DEPTH):
        wv = w_mem_kv[l][:, MEM_WIDTH:].reshape(D_MODEL, MEM_HEADS, HEAD_DIM)
        blocks = []
        for hm in range(MEM_HEADS):
            slot = MEM_HEADS - 1 if l < N_A else hm
            blk = jnp.zeros((D_MODEL, MEM_HEADS, HEAD_DIM), F32).at[:, slot].set(wv[:, hm])
            blocks.append(blk.reshape(D_MODEL, GROUP_LANES))
        wv_blocks.append(jnp.concatenate([w_mem_kv[l][:, :MEM_WIDTH]] + blocks, axis=1))
    w_memkv_pad = jnp.stack(wv_blocks).astype(BF16)
    gk_mem = jnp.tile(mem_k_norm, (1, MEM_HEADS)).reshape(DEPTH, 1, MEM_WIDTH)
    mkt, mvw = _mem_kv(mem, mem_norm.reshape(DEPTH, 1, D_MODEL), w_memkv_pad, gk_mem, seg_all)

    win_lanes = jnp.repeat(jnp.array(POOL_WINDOWS, F32), POOL_GROUP_DIM)
    wl = _cols_a(win_lanes, zeros_mem).reshape(1, D_MODEL)
    invw = _cols_a(1.0 / win_lanes, zeros_mem).reshape(1, D_MODEL)
    bias = _swa_bias()

    h = x
    kt = v = None
    for l in range(DEPTH):
        g_mix = norm_mix[l].reshape(1, D_MODEL)
        if l < N_A:
            w_in_p, w_out_p = _perm_in_out(w_in[l], w_out[l], _cols_a)
            gq = _cols_a(zeros_main, jnp.tile(mem_q_norm[l], MEM_HEADS)).reshape(1, D_MODEL)
            poolw = jnp.pad(pool_w[l], ((0, 0), (0, GROUP_LANES - POOL_GROUP_DIM),
                                        (0, GROUP_LANES - POOL_GROUP_DIM))).astype(BF16)
            pscale = _cols_a(pool_scale[l], zeros_mem).reshape(1, D_MODEL)
            h = _mix_a(l, h, g_mix, w_in_p, gq, seg_last, wl, invw, poolw, pscale, mkt, mvw, w_out_p)
        else:
            j = l - N_A
            if l == N_A:
                gk = jnp.tile(k_norm, SWA_KV_HEADS).reshape(1, KV_HALF)
                kt, v = _shared_kv(h, kv_norm.reshape(1, D_MODEL), w_kv.astype(BF16), gk, seg_all)
            w_in_p, w_out_p = _perm_in_out(w_in[l], w_out[l], _cols_b)
            gq = _cols_b(jnp.tile(q_norm[j], SWA_Q_HEADS), jnp.tile(mem_q_norm[l], MEM_HEADS))
            h = _mix_b(l, h, sinks[j], g_mix, w_in_p, gq.reshape(1, D_MODEL), seg_all, kt, v, bias,
                       mkt, mvw, w_out_p)
        h = _mlp(l, h.reshape(B * S, D_MODEL), norm_mlp[l].reshape(1, D_MODEL),
                 w_up[l].astype(BF16), w_down[l].astype(BF16)).reshape(B, S, D_MODEL)
    return h
```

```python
import jax
import jax.numpy as jnp
from jax import lax
from jax.experimental import pallas as pl
from jax.experimental.pallas import tpu as pltpu

F32 = jnp.float32
BF16 = jnp.bfloat16

D_MODEL = 1024
DEPTH = 4
N_A = DEPTH // 2
HEAD_DIM = 64
MEM_LEN = 256
MEM_HEADS = 4
MEM_WIDTH = MEM_HEADS * HEAD_DIM
MAIN_WIDTH = D_MODEL - MEM_WIDTH
POOL_WINDOWS = (2, 4, 8, 16)
POOL_GROUPS = len(POOL_WINDOWS)
POOL_GROUP_DIM = MAIN_WIDTH // POOL_GROUPS
SWA_Q_HEADS = MAIN_WIDTH // HEAD_DIM
SWA_KV_HEADS = 4
SWA_GROUP = SWA_Q_HEADS // SWA_KV_HEADS
KV_HALF = SWA_KV_HEADS * HEAD_DIM
BLOCK = 128
D_FF = 4 * D_MODEL
EPS = 1e-6
SCALE = HEAD_DIM ** -0.5
NEG = -1e30

LANES = 128
GROUP_LANES = 256
N_GROUPS = D_MODEL // GROUP_LANES
POOL_HALO = 16

TM_MIX = 512
TM_MLP = 512
TM_KV = 512
VMEM_LIMIT = 52 * 1024 * 1024


def _dot(a, b):
    return jnp.dot(a, b, preferred_element_type=F32)


def _rms(x, g):
    ms = jnp.mean(x * x, axis=-1, keepdims=True)
    return x * lax.rsqrt(ms + EPS) * g


def _seg_sumsq(x, seg):
    sq = x * x
    hi = sq.astype(BF16)
    lo = (sq - hi.astype(F32)).astype(BF16)
    return _dot(hi, seg) + _dot(lo, seg)


def _head_norm(x, seg, gain):
    ss = _seg_sumsq(x, seg)
    return x * lax.rsqrt(ss * (1.0 / HEAD_DIM) + EPS) * gain


def _half_rscale(sq, upper):
    lane = lax.broadcasted_iota(jnp.int32, (1, LANES), 1)
    keep = (lane >= HEAD_DIM) if upper else (lane < HEAD_DIM)
    ss = jnp.sum(jnp.where(keep, sq, 0.0), axis=-1, keepdims=True)
    return lax.rsqrt(ss * (1.0 / HEAD_DIM) + EPS) * SCALE


def _softmax_rows(s):
    m = jnp.max(s, axis=-1, keepdims=True)
    e = jnp.exp(s - m)
    return e * (1.0 / jnp.sum(e, axis=-1, keepdims=True))


def _rows_only(a, r0, r1, dst0=None, n=None):
    n = a.shape[0] if n is None else n
    dst0 = r0 if dst0 is None else dst0
    pieces = []
    if dst0 > 0:
        pieces.append(jnp.zeros((dst0, a.shape[1]), a.dtype))
    pieces.append(a[r0:r1])
    rest = n - dst0 - (r1 - r0)
    if rest > 0:
        pieces.append(jnp.zeros((rest, a.shape[1]), a.dtype))
    return jnp.concatenate(pieces, axis=0)


def _mem_kv_kernel(mem_ref, g_ref, w_ref, gk_ref, seg_ref, mkt_ref, mvw_ref):
    xn = _rms(mem_ref[0], g_ref[0]).astype(BF16)
    kv = _dot(xn, w_ref[0])
    kn = _head_norm(kv[:, :MEM_WIDTH], seg_ref[...], gk_ref[0])
    mkt_ref[0, 0] = kn.T.astype(BF16)
    mvw_ref[0, 0] = kv[:, MEM_WIDTH:].astype(BF16)


def _mem_kv(mem, mem_norm, w_memkv_pad, gk_mem, seg):
    B = mem.shape[0]
    wide = w_memkv_pad.shape[-1]
    return pl.pallas_call(
        _mem_kv_kernel,
        grid=(DEPTH, B),
        in_specs=[
            pl.BlockSpec((1, MEM_LEN, D_MODEL), lambda l, b: (b, 0, 0)),
            pl.BlockSpec((1, 1, D_MODEL), lambda l, b: (l, 0, 0)),
            pl.BlockSpec((1, D_MODEL, wide), lambda l, b: (l, 0, 0)),
            pl.BlockSpec((1, 1, MEM_WIDTH), lambda l, b: (l, 0, 0)),
            pl.BlockSpec((GROUP_LANES, GROUP_LANES), lambda l, b: (0, 0)),
        ],
        out_specs=[
            pl.BlockSpec((1, 1, MEM_WIDTH, MEM_LEN), lambda l, b: (l, b, 0, 0)),
            pl.BlockSpec((1, 1, MEM_LEN, wide - MEM_WIDTH), lambda l, b: (l, b, 0, 0)),
        ],
        out_shape=[
            jax.ShapeDtypeStruct((DEPTH, B, MEM_WIDTH, MEM_LEN), BF16),
            jax.ShapeDtypeStruct((DEPTH, B, MEM_LEN, wide - MEM_WIDTH), BF16),
        ],
        compiler_params=pltpu.CompilerParams(
            dimension_semantics=("arbitrary", "arbitrary"), vmem_limit_bytes=VMEM_LIMIT),
        name="mem_kv",
    )(mem, mem_norm, w_memkv_pad, gk_mem, seg)


def _kv_kernel(h_ref, g_ref, w_ref, gk_ref, seg_ref, kt_ref, v_ref):
    xn = _rms(h_ref[0], g_ref[...]).astype(BF16)
    kv = _dot(xn, w_ref[...])
    kn = _head_norm(kv[:, :KV_HALF], seg_ref[...], gk_ref[...])
    kt_ref[0] = kn.T.astype(BF16)
    v_ref[0] = kv[:, KV_HALF:].astype(BF16)


def _shared_kv(h, kv_norm, w_kv, gk, seg):
    B, S, _ = h.shape
    return pl.pallas_call(
        _kv_kernel,
        grid=(B, S // TM_KV),
        in_specs=[
            pl.BlockSpec((1, TM_KV, D_MODEL), lambda b, i: (b, i, 0)),
            pl.BlockSpec((1, D_MODEL), lambda b, i: (0, 0)),
            pl.BlockSpec((D_MODEL, 2 * KV_HALF), lambda b, i: (0, 0)),
            pl.BlockSpec((1, KV_HALF), lambda b, i: (0, 0)),
            pl.BlockSpec((GROUP_LANES, GROUP_LANES), lambda b, i: (0, 0)),
        ],
        out_specs=[
            pl.BlockSpec((1, KV_HALF, TM_KV), lambda b, i: (b, 0, i)),
            pl.BlockSpec((1, TM_KV, KV_HALF), lambda b, i: (b, i, 0)),
        ],
        out_shape=[
            jax.ShapeDtypeStruct((B, KV_HALF, S), BF16),
            jax.ShapeDtypeStruct((B, S, KV_HALF), BF16),
        ],
        compiler_params=pltpu.CompilerParams(
            dimension_semantics=("arbitrary", "arbitrary"), vmem_limit_bytes=VMEM_LIMIT),
        name="shared_kv",
    )(h, kv_norm, w_kv, gk, seg)


def _mix_a_kernel(h_ref, g_ref, win_ref, gq_ref, poolw_ref, pscale_ref, mkt_ref, mvw_ref, wout_ref,
                  o_ref, halo_ref, ext_ref, cat_ref):
    i = pl.program_id(1)
    tm = h_ref.shape[1]

    @pl.when(i == 0)
    def _():
        halo_ref[...] = jnp.zeros_like(halo_ref)

    xn = _rms(h_ref[0], g_ref[...]).astype(BF16)
    proj = _dot(xn, win_ref[...])
    ext_ref[0:POOL_HALO] = halo_ref[...]
    ext_ref[POOL_HALO:] = proj
    halo_ref[...] = proj[tm - POOL_HALO:]

    mkt = mkt_ref[0, 0]
    t1 = (lax.broadcasted_iota(jnp.int32, (POOL_HALO, GROUP_LANES), 0) + (i * tm + 1)).astype(F32)
    for g in range(N_GROUPS):
        lanes = slice(g * GROUP_LANES, (g + 1) * GROUP_LANES)
        s = ext_ref[:, lanes]
        for k in range(g + 1):
            s = s + pltpu.roll(s, 1 << k, 0)
        s = s[POOL_HALO:]
        win = float(POOL_WINDOWS[g])
        pooled = jnp.concatenate(
            [s[:POOL_HALO] / jnp.minimum(t1, win), s[POOL_HALO:] * (1.0 / win)], axis=0)
        d = (pooled - ext_ref[POOL_HALO:, lanes]).astype(BF16)
        mixed = _dot(d, poolw_ref[g]) * pscale_ref[:, lanes]

        mlanes = slice((g + 1) * GROUP_LANES - LANES, (g + 1) * GROUP_LANES)
        qc = ext_ref[POOL_HALO:, mlanes]
        r = _half_rscale(qc * qc, upper=True)
        qn = (qc * gq_ref[:, mlanes]).astype(BF16)
        ktm = _rows_only(mkt, g * HEAD_DIM, (g + 1) * HEAD_DIM, dst0=LANES - HEAD_DIM, n=LANES)
        p = _softmax_rows(_dot(qn, ktm) * r).astype(BF16)
        memo = _dot(p, mvw_ref[0, 0, :, lanes])
        cat_ref[:, lanes] = (mixed + memo).astype(BF16)
    o_ref[0] = h_ref[0] + _dot(cat_ref[...], wout_ref[...])


def _mix_a(layer, h, g, w_in, gq, poolw, pscale, mkt, mvw, w_out):
    B, S, _ = h.shape
    tm = TM_MIX
    const2 = lambda b, i: (0, 0)
    return pl.pallas_call(
        _mix_a_kernel,
        grid=(B, S // tm),
        in_specs=[
            pl.BlockSpec((1, tm, D_MODEL), lambda b, i: (b, i, 0)),
            pl.BlockSpec((1, D_MODEL), const2),
            pl.BlockSpec((D_MODEL, D_MODEL), const2),
            pl.BlockSpec((1, D_MODEL), const2),
            pl.BlockSpec((N_GROUPS, GROUP_LANES, GROUP_LANES), lambda b, i: (0, 0, 0)),
            pl.BlockSpec((1, D_MODEL), const2),
            pl.BlockSpec((1, 1, MEM_WIDTH, MEM_LEN), lambda b, i: (layer, b, 0, 0)),
            pl.BlockSpec((1, 1, MEM_LEN, D_MODEL), lambda b, i: (layer, b, 0, 0)),
            pl.BlockSpec((D_MODEL, D_MODEL), const2),
        ],
        out_specs=pl.BlockSpec((1, tm, D_MODEL), lambda b, i: (b, i, 0)),
        out_shape=jax.ShapeDtypeStruct(h.shape, F32),
        scratch_shapes=[
            pltpu.VMEM((POOL_HALO, D_MODEL), F32),
            pltpu.VMEM((tm + POOL_HALO, D_MODEL), F32),
            pltpu.VMEM((tm, D_MODEL), BF16),
        ],
        compiler_params=pltpu.CompilerParams(
            dimension_semantics=("arbitrary", "arbitrary"), vmem_limit_bytes=VMEM_LIMIT),
        name=f"mix_pool_{layer}",
    )(h, g, w_in, gq, poolw, pscale, mkt, mvw, w_out)


def _mix_b_kernel(sinks_ref, h_ref, g_ref, win_ref, gq_ref, ktc_ref, ktp_ref, vc_ref, vp_ref,
                  bias_ref, mkt_ref, mvw_ref, wout_ref, o_ref, proj_ref, cat_ref):
    i = pl.program_id(1)
    tm = h_ref.shape[1]
    nq = tm // BLOCK
    xn = _rms(h_ref[0], g_ref[...]).astype(BF16)
    proj_ref[...] = _dot(xn, win_ref[...])

    tri = (lax.broadcasted_iota(jnp.int32, (BLOCK, BLOCK), 0)
           >= lax.broadcasted_iota(jnp.int32, (BLOCK, BLOCK), 1))
    pen = jnp.where(tri, 0.0, jnp.where(i == 0, NEG, 0.0))

    lane_head = lax.broadcasted_iota(jnp.int32, (BLOCK, KV_HALF), 1) // HEAD_DIM
    vblocks = [vp_ref[0]] + [vc_ref[0, b * BLOCK:(b + 1) * BLOCK] for b in range(nq)]
    vmask = [[jnp.where(lane_head == kvh, vb, jnp.zeros_like(vb)) for kvh in range(SWA_KV_HEADS)]
             for vb in vblocks]

    mkt = mkt_ref[0, 0]
    mvst = jnp.concatenate(
        [mvw_ref[0, 0, :, hm * GROUP_LANES:(hm + 1) * GROUP_LANES] for hm in range(MEM_HEADS)], axis=0)
    mem_lanes = slice(SWA_GROUP * GROUP_LANES, D_MODEL)

    for qb in range(nq):
        r0 = qb * BLOCK
        rows = slice(r0, r0 + BLOCK)
        if qb == 0:
            ktw = jnp.concatenate([ktp_ref[0], ktc_ref[0, :, 0:BLOCK]], axis=1)
        else:
            ktw = ktc_ref[0, :, r0 - BLOCK:r0 + BLOCK]
        pblk = proj_ref[rows, 0:MAIN_WIDTH]
        rcol = []
        for c in range(MAIN_WIDTH // LANES):
            pc = pblk[:, c * LANES:(c + 1) * LANES]
            sq = pc * pc
            rcol.append(_half_rscale(sq, upper=False))
            rcol.append(_half_rscale(sq, upper=True))
        qn = (pblk * gq_ref[:, 0:MAIN_WIDTH]).astype(BF16)
        qst = jnp.concatenate(
            [qn[:, g * GROUP_LANES:(g + 1) * GROUP_LANES] for g in range(SWA_GROUP)], axis=0)
        p_all = []
        v_all = []
        for kvh in range(SWA_KV_HEADS):
            ktm = _rows_only(ktw, kvh * HEAD_DIM, (kvh + 1) * HEAD_DIM)
            s_all = _dot(qst, ktm)
            ps = []
            for g in range(SWA_GROUP):
                hq = kvh * SWA_GROUP + g
                sg = s_all[g * BLOCK:(g + 1) * BLOCK]
                s = jnp.where(tri, sg[:, BLOCK:], sg[:, :BLOCK]) * rcol[g * SWA_KV_HEADS + kvh] + bias_ref[hq]
                if qb == 0:
                    s = s + pen
                sink = sinks_ref[hq]
                m = jnp.maximum(jnp.max(s, axis=-1, keepdims=True), sink)
                e = jnp.exp(s - m)
                den = jnp.sum(e, axis=-1, keepdims=True) + jnp.exp(sink - m)
                p = e * (1.0 / den)
                ps.append(jnp.concatenate(
                    [jnp.where(tri, 0.0, p).astype(BF16), jnp.where(tri, p, 0.0).astype(BF16)], axis=1))
            p_all.append(jnp.concatenate(ps, axis=0))
            v_all.append(jnp.concatenate([vmask[qb][kvh], vmask[qb + 1][kvh]], axis=0))
        o = _dot(jnp.concatenate(p_all, axis=1), jnp.concatenate(v_all, axis=0))
        for g in range(SWA_GROUP):
            cat_ref[rows, g * GROUP_LANES:(g + 1) * GROUP_LANES] = o[g * BLOCK:(g + 1) * BLOCK].astype(BF16)

    pmem = proj_ref[:, mem_lanes]
    qm = (pmem * gq_ref[:, mem_lanes]).astype(BF16)
    pm = []
    for hm in range(MEM_HEADS):
        pc = pmem[:, (hm // 2) * LANES:(hm // 2 + 1) * LANES]
        r = _half_rscale(pc * pc, upper=bool(hm % 2))
        ktm = _rows_only(mkt, hm * HEAD_DIM, (hm + 1) * HEAD_DIM)
        pm.append(_softmax_rows(_dot(qm, ktm) * r).astype(BF16))
    cat_ref[:, mem_lanes] = _dot(jnp.concatenate(pm, axis=1), mvst).astype(BF16)

    o_ref[0] = h_ref[0] + _dot(cat_ref[...], wout_ref[...])


def _mix_b(layer, h, sinks, g, w_in, gq, kt, v, bias, mkt, mvw, w_out):
    B, S, _ = h.shape
    tm = TM_MIX
    nq = tm // BLOCK
    const2 = lambda b, i: (0, 0)
    return pl.pallas_call(
        _mix_b_kernel,
        grid=(B, S // tm),
        in_specs=[
            pl.BlockSpec(memory_space=pltpu.SMEM),
            pl.BlockSpec((1, tm, D_MODEL), lambda b, i: (b, i, 0)),
            pl.BlockSpec((1, D_MODEL), const2),
            pl.BlockSpec((D_MODEL, D_MODEL), const2),
            pl.BlockSpec((1, D_MODEL), const2),
            pl.BlockSpec((1, KV_HALF, tm), lambda b, i: (b, 0, i)),
            pl.BlockSpec((1, KV_HALF, BLOCK), lambda b, i: (b, 0, jnp.maximum(i * nq - 1, 0))),
            pl.BlockSpec((1, tm, KV_HALF), lambda b, i: (b, i, 0)),
            pl.BlockSpec((1, BLOCK, KV_HALF), lambda b, i: (b, jnp.maximum(i * nq - 1, 0), 0)),
            pl.BlockSpec((SWA_Q_HEADS, BLOCK, BLOCK), lambda b, i: (0, 0, 0)),
            pl.BlockSpec((1, 1, MEM_WIDTH, MEM_LEN), lambda b, i: (layer, b, 0, 0)),
            pl.BlockSpec((1, 1, MEM_LEN, D_MODEL), lambda b, i: (layer, b, 0, 0)),
            pl.BlockSpec((D_MODEL, D_MODEL), const2),
        ],
        out_specs=pl.BlockSpec((1, tm, D_MODEL), lambda b, i: (b, i, 0)),
        out_shape=jax.ShapeDtypeStruct(h.shape, F32),
        scratch_shapes=[pltpu.VMEM((tm, D_MODEL), F32), pltpu.VMEM((tm, D_MODEL), BF16)],
        compiler_params=pltpu.CompilerParams(
            dimension_semantics=("arbitrary", "arbitrary"), vmem_limit_bytes=VMEM_LIMIT),
        name=f"mix_swa_{layer}",
    )(sinks, h, g, w_in, gq, kt, kt, v, v, bias, mkt, mvw, w_out)


def _mlp_kernel(h_ref, g_ref, wup_ref, wdn_ref, o_ref):
    x = h_ref[...]
    xn = _rms(x, g_ref[...]).astype(BF16)
    u = jnp.maximum(_dot(xn, wup_ref[...]), 0.0)
    o_ref[...] = x + _dot((u * u).astype(BF16), wdn_ref[...])


def _mlp(layer, h2d, g, w_up, w_down):
    T = h2d.shape[0]
    tm = TM_MLP
    return pl.pallas_call(
        _mlp_kernel,
        grid=(T // tm,),
        in_specs=[
            pl.BlockSpec((tm, D_MODEL), lambda i: (i, 0)),
            pl.BlockSpec((1, D_MODEL), lambda i: (0, 0)),
            pl.BlockSpec((D_MODEL, D_FF), lambda i: (0, 0), pipeline_mode=pl.Buffered(1)),
            pl.BlockSpec((D_FF, D_MODEL), lambda i: (0, 0), pipeline_mode=pl.Buffered(1)),
        ],
        out_specs=pl.BlockSpec((tm, D_MODEL), lambda i: (i, 0)),
        out_shape=jax.ShapeDtypeStruct(h2d.shape, F32),
        compiler_params=pltpu.CompilerParams(
            dimension_semantics=("arbitrary",), vmem_limit_bytes=VMEM_LIMIT),
        name=f"mlp_{layer}",
    )(h2d, g, w_up, w_down)


def _cols_a(main, memq):
    lead = main.shape[:-1]
    m = main.reshape(*lead, POOL_GROUPS, POOL_GROUP_DIM)
    q = memq.reshape(*lead, MEM_HEADS, HEAD_DIM)
    return jnp.concatenate([m, q], axis=-1).reshape(*lead, D_MODEL)


def _cols_b(main, memq):
    lead = main.shape[:-1]
    m = main.reshape(*lead, SWA_KV_HEADS, SWA_GROUP, HEAD_DIM)
    m = jnp.swapaxes(m, -3, -2).reshape(*lead, MAIN_WIDTH)
    return jnp.concatenate([m, memq], axis=-1)


def _perm_in_out(w_in_l, w_out_l, cols):
    w_in_p = cols(w_in_l[:, :MAIN_WIDTH], w_in_l[:, MAIN_WIDTH:])
    w_out_t = w_out_l.T
    w_out_p = cols(w_out_t[:, :MAIN_WIDTH], w_out_t[:, MAIN_WIDTH:]).T
    return w_in_p.astype(BF16), w_out_p.astype(BF16)


def _block_diag_ones():
    idx = jnp.arange(GROUP_LANES) // HEAD_DIM
    return (idx[:, None] == idx[None, :]).astype(BF16)


def _swa_bias():
    qi = jnp.arange(BLOCK, dtype=jnp.int32)[:, None]
    kj = jnp.arange(BLOCK, dtype=jnp.int32)[None, :]
    dist = jnp.where(kj <= qi, qi - kj, qi - kj + BLOCK)
    slopes = jnp.exp2(-8.0 * jnp.arange(1, SWA_Q_HEADS + 1, dtype=F32) / SWA_Q_HEADS)
    return -(slopes[:, None, None] * dist.astype(F32)[None])


def kernel(x, mem, norm_mix, w_in, pool_w, pool_scale, kv_norm, w_kv, k_norm, q_norm, sinks, mem_norm,
           w_mem_kv, mem_q_norm, mem_k_norm, w_out, norm_mlp, w_up, w_down):
    B, S, _ = x.shape
    zeros_main = jnp.zeros((MAIN_WIDTH,), F32)
    zeros_mem = jnp.zeros((MEM_WIDTH,), F32)
    seg = _block_diag_ones()

    wv_blocks = []
    for l in range(DEPTH):
        wv = w_mem_kv[l][:, MEM_WIDTH:].reshape(D_MODEL, MEM_HEADS, HEAD_DIM)
        blocks = []
        for hm in range(MEM_HEADS):
            slot = MEM_HEADS - 1 if l < N_A else hm
            blk = jnp.zeros((D_MODEL, MEM_HEADS, HEAD_DIM), F32).at[:, slot].set(wv[:, hm])
            blocks.append(blk.reshape(D_MODEL, GROUP_LANES))
        wv_blocks.append(jnp.concatenate([w_mem_kv[l][:, :MEM_WIDTH]] + blocks, axis=1))
    w_memkv_pad = jnp.stack(wv_blocks).astype(BF16)
    gk_mem = jnp.tile(mem_k_norm, (1, MEM_HEADS)).reshape(DEPTH, 1, MEM_WIDTH)
    mkt, mvw = _mem_kv(mem, mem_norm.reshape(DEPTH, 1, D_MODEL), w_memkv_pad, gk_mem, seg)

    bias = _swa_bias()

    h = x
    kt = v = None
    for l in range(DEPTH):
        g_mix = norm_mix[l].reshape(1, D_MODEL)
        if l < N_A:
            w_in_p, w_out_p = _perm_in_out(w_in[l], w_out[l], _cols_a)
            gq = _cols_a(zeros_main, jnp.tile(mem_q_norm[l], MEM_HEADS)).reshape(1, D_MODEL)
            poolw = jnp.pad(pool_w[l], ((0, 0), (0, GROUP_LANES - POOL_GROUP_DIM),
                                        (0, GROUP_LANES - POOL_GROUP_DIM))).astype(BF16)
            pscale = _cols_a(pool_scale[l], zeros_mem).reshape(1, D_MODEL)
            h = _mix_a(l, h, g_mix, w_in_p, gq, poolw, pscale, mkt, mvw, w_out_p)
        else:
            j = l - N_A
            if l == N_A:
                gk = jnp.tile(k_norm, SWA_KV_HEADS).reshape(1, KV_HALF)
                kt, v = _shared_kv(h, kv_norm.reshape(1, D_MODEL), w_kv.astype(BF16), gk, seg)
            w_in_p, w_out_p = _perm_in_out(w_in[l], w_out[l], _cols_b)
            gq = _cols_b(jnp.tile(q_norm[j], SWA_Q_HEADS), jnp.tile(mem_q_norm[l], MEM_HEADS))
            h = _mix_b(l, h, sinks[j], g_mix, w_in_p, gq.reshape(1, D_MODEL), kt, v, bias,
                       mkt, mvw, w_out_p)
        h = _mlp(l, h.reshape(B * S, D_MODEL), norm_mlp[l].reshape(1, D_MODEL),
                 w_up[l].astype(BF16), w_down[l].astype(BF16)).reshape(B, S, D_MODEL)
    return h
```

```python
import jax
import jax.numpy as jnp
from jax import lax
from jax.experimental import pallas as pl
from jax.experimental.pallas import tpu as pltpu

F32 = jnp.float32
BF16 = jnp.bfloat16

D_MODEL = 1024
DEPTH = 4
N_A = DEPTH // 2
HEAD_DIM = 64
MEM_LEN = 256
MEM_HEADS = 4
MEM_WIDTH = MEM_HEADS * HEAD_DIM
MAIN_WIDTH = D_MODEL - MEM_WIDTH
POOL_WINDOWS = (2, 4, 8, 16)
POOL_GROUPS = len(POOL_WINDOWS)
POOL_GROUP_DIM = MAIN_WIDTH // POOL_GROUPS
SWA_Q_HEADS = MAIN_WIDTH // HEAD_DIM
SWA_KV_HEADS = 4
SWA_GROUP = SWA_Q_HEADS // SWA_KV_HEADS
KV_HALF = SWA_KV_HEADS * HEAD_DIM
BLOCK = 128
D_FF = 4 * D_MODEL
EPS = 1e-6
SCALE = HEAD_DIM ** -0.5
NEG = -1e30

LANES = 128
GROUP_LANES = 256
N_GROUPS = D_MODEL // GROUP_LANES
POOL_HALO = 16

TM_MIX = 1024
SUB_MIX = 256
TM_MLP = 512
KV_CHUNK = 256
VMEM_LIMIT = 52 * 1024 * 1024


def _dot(a, b):
    return jnp.dot(a, b, preferred_element_type=F32)


def _rms(x, g):
    ms = jnp.mean(x * x, axis=-1, keepdims=True)
    return x * lax.rsqrt(ms + EPS) * g


def _half_rms_factor(sq, upper):
    lane = lax.broadcasted_iota(jnp.int32, (1, LANES), 1)
    keep = (lane >= HEAD_DIM) if upper else (lane < HEAD_DIM)
    ss = jnp.sum(jnp.where(keep, sq, 0.0), axis=-1, keepdims=True)
    return lax.rsqrt(ss * (1.0 / HEAD_DIM) + EPS)


def _half_rscale(sq, upper):
    return _half_rms_factor(sq, upper) * SCALE


def _head_norm(x, gain):
    lower = lax.broadcasted_iota(jnp.int32, (1, LANES), 1) < HEAD_DIM
    cols = []
    for c in range(x.shape[1] // LANES):
        lanes = slice(c * LANES, (c + 1) * LANES)
        xc = x[:, lanes]
        sq = xc * xc
        r = jnp.where(lower, _half_rms_factor(sq, upper=False), _half_rms_factor(sq, upper=True))
        cols.append(xc * r * gain[:, lanes])
    return jnp.concatenate(cols, axis=1)


def _softmax_rows(s):
    m = jnp.max(s, axis=-1, keepdims=True)
    e = jnp.exp(s - m)
    return e * (1.0 / jnp.sum(e, axis=-1, keepdims=True))


def _rows_only(a, r0, r1, dst0=None, n=None):
    n = a.shape[0] if n is None else n
    dst0 = r0 if dst0 is None else dst0
    pieces = []
    if dst0 > 0:
        pieces.append(jnp.zeros((dst0, a.shape[1]), a.dtype))
    pieces.append(a[r0:r1])
    rest = n - dst0 - (r1 - r0)
    if rest > 0:
        pieces.append(jnp.zeros((rest, a.shape[1]), a.dtype))
    return jnp.concatenate(pieces, axis=0)


def _mem_kv_kernel(mem_ref, g_ref, w_ref, gk_ref, mkt_ref, mvw_ref):
    xn = _rms(mem_ref[0], g_ref[0]).astype(BF16)
    kv = _dot(xn, w_ref[0])
    kn = _head_norm(kv[:, :MEM_WIDTH], gk_ref[0])
    mkt_ref[0, 0] = kn.T.astype(BF16)
    mvw_ref[0, 0] = kv[:, MEM_WIDTH:].astype(BF16)


def _mem_kv(mem, mem_norm, w_memkv_pad, gk_mem):
    B = mem.shape[0]
    wide = w_memkv_pad.shape[-1]
    return pl.pallas_call(
        _mem_kv_kernel,
        grid=(DEPTH, B),
        in_specs=[
            pl.BlockSpec((1, MEM_LEN, D_MODEL), lambda l, b: (b, 0, 0)),
            pl.BlockSpec((1, 1, D_MODEL), lambda l, b: (l, 0, 0)),
            pl.BlockSpec((1, D_MODEL, wide), lambda l, b: (l, 0, 0)),
            pl.BlockSpec((1, 1, MEM_WIDTH), lambda l, b: (l, 0, 0)),
        ],
        out_specs=[
            pl.BlockSpec((1, 1, MEM_WIDTH, MEM_LEN), lambda l, b: (l, b, 0, 0)),
            pl.BlockSpec((1, 1, MEM_LEN, wide - MEM_WIDTH), lambda l, b: (l, b, 0, 0)),
        ],
        out_shape=[
            jax.ShapeDtypeStruct((DEPTH, B, MEM_WIDTH, MEM_LEN), BF16),
            jax.ShapeDtypeStruct((DEPTH, B, MEM_LEN, wide - MEM_WIDTH), BF16),
        ],
        compiler_params=pltpu.CompilerParams(
            dimension_semantics=("arbitrary", "arbitrary"), vmem_limit_bytes=VMEM_LIMIT),
        name="mem_kv",
    )(mem, mem_norm, w_memkv_pad, gk_mem)


def _mix_a_kernel(h_ref, g_ref, win_ref, gq_ref, poolw_ref, pscale_ref, mkt_ref, mvw_ref, wout_ref,
                  o_ref, halo_ref, ext_ref, cat_ref):
    i = pl.program_id(1)
    tm = h_ref.shape[1]

    @pl.when(i == 0)
    def _():
        halo_ref[...] = jnp.zeros_like(halo_ref)

    xn = _rms(h_ref[0], g_ref[...]).astype(BF16)
    proj = _dot(xn, win_ref[...])
    ext_ref[0:POOL_HALO] = halo_ref[...]
    ext_ref[POOL_HALO:] = proj
    halo_ref[...] = proj[tm - POOL_HALO:]

    mkt = mkt_ref[0, 0]
    t1 = (lax.broadcasted_iota(jnp.int32, (POOL_HALO, GROUP_LANES), 0) + (i * tm + 1)).astype(F32)
    for g in range(N_GROUPS):
        lanes = slice(g * GROUP_LANES, (g + 1) * GROUP_LANES)
        s = ext_ref[:, lanes]
        for k in range(g + 1):
            s = s + pltpu.roll(s, 1 << k, 0)
        s = s[POOL_HALO:]
        win = float(POOL_WINDOWS[g])
        pooled = jnp.concatenate(
            [s[:POOL_HALO] / jnp.minimum(t1, win), s[POOL_HALO:] * (1.0 / win)], axis=0)
        d = (pooled - ext_ref[POOL_HALO:, lanes]).astype(BF16)
        mixed = _dot(d, poolw_ref[g]) * pscale_ref[:, lanes]

        mlanes = slice((g + 1) * GROUP_LANES - LANES, (g + 1) * GROUP_LANES)
        qc = ext_ref[POOL_HALO:, mlanes]
        r = _half_rscale(qc * qc, upper=True)
        qn = (qc * gq_ref[:, mlanes]).astype(BF16)
        ktm = _rows_only(mkt, g * HEAD_DIM, (g + 1) * HEAD_DIM, dst0=LANES - HEAD_DIM, n=LANES)
        p = _softmax_rows(_dot(qn, ktm) * r).astype(BF16)
        memo = _dot(p, mvw_ref[0, 0, :, lanes])
        cat_ref[:, lanes] = (mixed + memo).astype(BF16)
    o_ref[0] = h_ref[0] + _dot(cat_ref[...], wout_ref[...])


def _mix_a(layer, h, g, w_in, gq, poolw, pscale, mkt, mvw, w_out):
    B, S, _ = h.shape
    tm = TM_MIX
    const2 = lambda b, i: (0, 0)
    return pl.pallas_call(
        _mix_a_kernel,
        grid=(B, S // tm),
        in_specs=[
            pl.BlockSpec((1, tm, D_MODEL), lambda b, i: (b, i, 0)),
            pl.BlockSpec((1, D_MODEL), const2),
            pl.BlockSpec((D_MODEL, D_MODEL), const2),
            pl.BlockSpec((1, D_MODEL), const2),
            pl.BlockSpec((N_GROUPS, GROUP_LANES, GROUP_LANES), lambda b, i: (0, 0, 0)),
            pl.BlockSpec((1, D_MODEL), const2),
            pl.BlockSpec((1, 1, MEM_WIDTH, MEM_LEN), lambda b, i: (layer, b, 0, 0)),
            pl.BlockSpec((1, 1, MEM_LEN, D_MODEL), lambda b, i: (layer, b, 0, 0)),
            pl.BlockSpec((D_MODEL, D_MODEL), const2),
        ],
        out_specs=pl.BlockSpec((1, tm, D_MODEL), lambda b, i: (b, i, 0)),
        out_shape=jax.ShapeDtypeStruct(h.shape, F32),
        scratch_shapes=[
            pltpu.VMEM((POOL_HALO, D_MODEL), F32),
            pltpu.VMEM((tm + POOL_HALO, D_MODEL), F32),
            pltpu.VMEM((tm, D_MODEL), BF16),
        ],
        compiler_params=pltpu.CompilerParams(
            dimension_semantics=("arbitrary", "arbitrary"), vmem_limit_bytes=VMEM_LIMIT),
        name=f"mix_pool_{layer}",
    )(h, g, w_in, gq, poolw, pscale, mkt, mvw, w_out)


def _mix_b_kernel(sinks_ref, h_ref, g_ref, win_ref, gq_ref, ktc_ref, ktp_ref, vc_ref, vp_ref,
                  bias_ref, mkt_ref, mvw_ref, wout_ref, o_ref, xn_ref, proj_ref, cat_ref):
    i = pl.program_id(1)
    tm = h_ref.shape[1]
    nq = tm // BLOCK

    tri = (lax.broadcasted_iota(jnp.int32, (BLOCK, BLOCK), 0)
           >= lax.broadcasted_iota(jnp.int32, (BLOCK, BLOCK), 1))
    pen = jnp.where(tri, 0.0, jnp.where(i == 0, NEG, 0.0))

    lane_head = lax.broadcasted_iota(jnp.int32, (BLOCK, KV_HALF), 1) // HEAD_DIM
    vblocks = [vp_ref[...]] + [vc_ref[b * BLOCK:(b + 1) * BLOCK] for b in range(nq)]
    vmask = [[jnp.where(lane_head == kvh, vb, jnp.zeros_like(vb)) for kvh in range(SWA_KV_HEADS)]
             for vb in vblocks]

    mkt = mkt_ref[0, 0]
    mvst = jnp.concatenate(
        [mvw_ref[0, 0, :, hm * GROUP_LANES:(hm + 1) * GROUP_LANES] for hm in range(MEM_HEADS)], axis=0)
    mem_lanes = slice(SWA_GROUP * GROUP_LANES, D_MODEL)

    def win_piece(sb, c):
        srows = slice(sb * SUB_MIX, (sb + 1) * SUB_MIX)
        lanes = slice(c * GROUP_LANES, (c + 1) * GROUP_LANES)
        if c == 0:
            xn_ref[srows] = _rms(h_ref[0, srows], g_ref[...]).astype(BF16)
        proj_ref[srows, lanes] = _dot(xn_ref[srows], win_ref[:, lanes])

    def wout_piece(sb, c):
        srows = slice(sb * SUB_MIX, (sb + 1) * SUB_MIX)
        lanes = slice(c * GROUP_LANES, (c + 1) * GROUP_LANES)
        o_ref[0, srows, lanes] = h_ref[0, srows, lanes] + _dot(cat_ref[srows], wout_ref[:, lanes])

    def swa_block(qb):
        r0 = qb * BLOCK
        rows = slice(r0, r0 + BLOCK)
        if qb == 0:
            ktw = jnp.concatenate([ktp_ref[...], ktc_ref[:, 0:BLOCK]], axis=1)
        else:
            ktw = ktc_ref[:, r0 - BLOCK:r0 + BLOCK]
        pblk = proj_ref[rows, 0:MAIN_WIDTH]
        rcol = []
        for c in range(MAIN_WIDTH // LANES):
            pc = pblk[:, c * LANES:(c + 1) * LANES]
            sq = pc * pc
            rcol.append(_half_rscale(sq, upper=False))
            rcol.append(_half_rscale(sq, upper=True))
        qn = (pblk * gq_ref[:, 0:MAIN_WIDTH]).astype(BF16)
        qst = jnp.concatenate(
            [qn[:, g * GROUP_LANES:(g + 1) * GROUP_LANES] for g in range(SWA_GROUP)], axis=0)
        p_all = []
        v_all = []
        for kvh in range(SWA_KV_HEADS):
            ktm = _rows_only(ktw, kvh * HEAD_DIM, (kvh + 1) * HEAD_DIM)
            s_all = _dot(qst, ktm)
            ps = []
            for g in range(SWA_GROUP):
                hq = kvh * SWA_GROUP + g
                sg = s_all[g * BLOCK:(g + 1) * BLOCK]
                s = (jnp.where(tri, sg[:, BLOCK:], sg[:, :BLOCK]) * rcol[g * SWA_KV_HEADS + kvh]
                     + bias_ref[hq])
                if qb == 0:
                    s = s + pen
                sink = sinks_ref[hq]
                m = jnp.maximum(jnp.max(s, axis=-1, keepdims=True), sink)
                e = jnp.exp(s - m)
                den = jnp.sum(e, axis=-1, keepdims=True) + jnp.exp(sink - m)
                p = e * (1.0 / den)
                ps.append(jnp.concatenate(
                    [jnp.where(tri, 0.0, p).astype(BF16), jnp.where(tri, p, 0.0).astype(BF16)], axis=1))
            p_all.append(jnp.concatenate(ps, axis=0))
            v_all.append(jnp.concatenate([vmask[qb][kvh], vmask[qb + 1][kvh]], axis=0))
        o = _dot(jnp.concatenate(p_all, axis=1), jnp.concatenate(v_all, axis=0))
        for g in range(SWA_GROUP):
            cat_ref[rows, g * GROUP_LANES:(g + 1) * GROUP_LANES] = o[g * BLOCK:(g + 1) * BLOCK].astype(BF16)

    def mem_block(sb):
        srows = slice(sb * SUB_MIX, (sb + 1) * SUB_MIX)
        pmem = proj_ref[srows, mem_lanes]
        qm = (pmem * gq_ref[:, mem_lanes]).astype(BF16)
        pm = []
        for hm in range(MEM_HEADS):
            pc = pmem[:, (hm // 2) * LANES:(hm // 2 + 1) * LANES]
            r = _half_rscale(pc * pc, upper=bool(hm % 2))
            ktm = _rows_only(mkt, hm * HEAD_DIM, (hm + 1) * HEAD_DIM)
            pm.append(_softmax_rows(_dot(qm, ktm) * r).astype(BF16))
        cat_ref[srows, mem_lanes] = _dot(jnp.concatenate(pm, axis=1), mvst).astype(BF16)

    nsub = tm // SUB_MIX
    qps = SUB_MIX // BLOCK
    for c in range(N_GROUPS):
        win_piece(0, c)
    for sb in range(nsub):
        main = [(swa_block, qb) for qb in range(sb * qps, (sb + 1) * qps)] + [(mem_block, sb)]
        fill = []
        if sb + 1 < nsub:
            fill += [(win_piece, sb + 1, c) for c in range(N_GROUPS)]
        if sb >= 1:
            fill += [(wout_piece, sb - 1, c) for c in range(N_GROUPS)]
        per = -(-len(fill) // len(main))
        for k, item in enumerate(main):
            for f in fill[k * per:(k + 1) * per]:
                f[0](*f[1:])
            item[0](*item[1:])
    for c in range(N_GROUPS):
        wout_piece(nsub - 1, c)


def _mix_b(layer, h, sinks, g, w_in, gq, kt, v, bias, mkt, mvw, w_out):
    B, S, _ = h.shape
    tm = TM_MIX
    nq = tm // BLOCK
    nt = S // tm
    const2 = lambda b, i: (0, 0)
    return pl.pallas_call(
        _mix_b_kernel,
        grid=(B, S // tm),
        in_specs=[
            pl.BlockSpec(memory_space=pltpu.SMEM),
            pl.BlockSpec((1, tm, D_MODEL), lambda b, i: (b, i, 0)),
            pl.BlockSpec((1, D_MODEL), const2),
            pl.BlockSpec((D_MODEL, D_MODEL), const2),
            pl.BlockSpec((1, D_MODEL), const2),
            pl.BlockSpec((KV_HALF, tm), lambda b, i: (0, b * nt + i)),
            pl.BlockSpec((KV_HALF, BLOCK), lambda b, i: (0, b * nt * nq + jnp.maximum(i * nq - 1, 0))),
            pl.BlockSpec((tm, KV_HALF), lambda b, i: (b * nt + i, 0)),
            pl.BlockSpec((BLOCK, KV_HALF), lambda b, i: (b * nt * nq + jnp.maximum(i * nq - 1, 0), 0)),
            pl.BlockSpec((SWA_Q_HEADS, BLOCK, BLOCK), lambda b, i: (0, 0, 0)),
            pl.BlockSpec((1, 1, MEM_WIDTH, MEM_LEN), lambda b, i: (layer, b, 0, 0)),
            pl.BlockSpec((1, 1, MEM_LEN, D_MODEL), lambda b, i: (layer, b, 0, 0)),
            pl.BlockSpec((D_MODEL, D_MODEL), const2),
        ],
        out_specs=pl.BlockSpec((1, tm, D_MODEL), lambda b, i: (b, i, 0)),
        out_shape=jax.ShapeDtypeStruct(h.shape, F32),
        scratch_shapes=[pltpu.VMEM((tm, D_MODEL), BF16), pltpu.VMEM((tm, D_MODEL), F32),
                        pltpu.VMEM((tm, D_MODEL), BF16)],
        compiler_params=pltpu.CompilerParams(
            dimension_semantics=("arbitrary", "arbitrary"), vmem_limit_bytes=VMEM_LIMIT),
        name=f"mix_swa_{layer}",
    )(sinks, h, g, w_in, gq, kt, kt, v, v, bias, mkt, mvw, w_out)


def _mlp_tile(h_ref, g_ref, wup_ref, wdn_ref):
    x = h_ref[...]
    xn = _rms(x, g_ref[...]).astype(BF16)
    u = jnp.maximum(_dot(xn, wup_ref[...]), 0.0)
    return x + _dot((u * u).astype(BF16), wdn_ref[...])


def _mlp_kernel(h_ref, g_ref, wup_ref, wdn_ref, o_ref):
    o_ref[...] = _mlp_tile(h_ref, g_ref, wup_ref, wdn_ref)


def _mlp_kv_kernel(h_ref, g_ref, wup_ref, wdn_ref, gkv_ref, wkv_ref, gk_ref, o_ref, kt_ref, v_ref):
    out = _mlp_tile(h_ref, g_ref, wup_ref, wdn_ref)
    o_ref[...] = out
    for r0 in range(0, out.shape[0], KV_CHUNK):
        rows = slice(r0, r0 + KV_CHUNK)
        kv = _dot(_rms(out[rows], gkv_ref[...]).astype(BF16), wkv_ref[...])
        kn = _head_norm(kv[:, :KV_HALF], gk_ref[...])
        kt_ref[:, rows] = kn.T.astype(BF16)
        v_ref[rows] = kv[:, KV_HALF:].astype(BF16)


def _mlp(layer, h2d, g, w_up, w_down, kv_params=None):
    T = h2d.shape[0]
    tm = TM_MLP
    const = lambda i: (0, 0)
    in_specs = [
        pl.BlockSpec((tm, D_MODEL), lambda i: (i, 0)),
        pl.BlockSpec((1, D_MODEL), const),
        pl.BlockSpec((D_MODEL, D_FF), const, pipeline_mode=pl.Buffered(1)),
        pl.BlockSpec((D_FF, D_MODEL), const, pipeline_mode=pl.Buffered(1)),
    ]
    out_specs = pl.BlockSpec((tm, D_MODEL), lambda i: (i, 0))
    out_shape = jax.ShapeDtypeStruct(h2d.shape, F32)
    args = (h2d, g, w_up, w_down)
    body = _mlp_kernel
    if kv_params is not None:
        in_specs += [
            pl.BlockSpec((1, D_MODEL), const),
            pl.BlockSpec((D_MODEL, 2 * KV_HALF), const, pipeline_mode=pl.Buffered(1)),
            pl.BlockSpec((1, KV_HALF), const),
        ]
        out_specs = [out_specs, pl.BlockSpec((KV_HALF, tm), lambda i: (0, i)),
                     pl.BlockSpec((tm, KV_HALF), lambda i: (i, 0))]
        out_shape = [out_shape, jax.ShapeDtypeStruct((KV_HALF, T), BF16),
                     jax.ShapeDtypeStruct((T, KV_HALF), BF16)]
        args += tuple(kv_params)
        body = _mlp_kv_kernel
    return pl.pallas_call(
        body,
        grid=(T // tm,),
        in_specs=in_specs,
        out_specs=out_specs,
        out_shape=out_shape,
        compiler_params=pltpu.CompilerParams(
            dimension_semantics=("arbitrary",), vmem_limit_bytes=VMEM_LIMIT),
        name=f"mlp_{layer}",
    )(*args)


def _cols_a(main, memq):
    lead = main.shape[:-1]
    m = main.reshape(*lead, POOL_GROUPS, POOL_GROUP_DIM)
    q = memq.reshape(*lead, MEM_HEADS, HEAD_DIM)
    return jnp.concatenate([m, q], axis=-1).reshape(*lead, D_MODEL)


def _cols_b(main, memq):
    lead = main.shape[:-1]
    m = main.reshape(*lead, SWA_KV_HEADS, SWA_GROUP, HEAD_DIM)
    m = jnp.swapaxes(m, -3, -2).reshape(*lead, MAIN_WIDTH)
    return jnp.concatenate([m, memq], axis=-1)


def _perm_in_out(w_in_l, w_out_l, cols):
    w_in_p = cols(w_in_l[:, :MAIN_WIDTH], w_in_l[:, MAIN_WIDTH:])
    w_out_t = w_out_l.T
    w_out_p = cols(w_out_t[:, :MAIN_WIDTH], w_out_t[:, MAIN_WIDTH:]).T
    return w_in_p.astype(BF16), w_out_p.astype(BF16)


def _swa_bias():
    qi = jnp.arange(BLOCK, dtype=jnp.int32)[:, None]
    kj = jnp.arange(BLOCK, dtype=jnp.int32)[None, :]
    dist = jnp.where(kj <= qi, qi - kj, qi - kj + BLOCK)
    slopes = jnp.exp2(-8.0 * jnp.arange(1, SWA_Q_HEADS + 1, dtype=F32) / SWA_Q_HEADS)
    return -(slopes[:, None, None] * dist.astype(F32)[None])


def kernel(x, mem, norm_mix, w_in, pool_w, pool_scale, kv_norm, w_kv, k_norm, q_norm, sinks, mem_norm,
           w_mem_kv, mem_q_norm, mem_k_norm, w_out, norm_mlp, w_up, w_down):
    B, S, _ = x.shape
    zeros_main = jnp.zeros((MAIN_WIDTH,), F32)
    zeros_mem = jnp.zeros((MEM_WIDTH,), F32)

    wv_blocks = []
    for l in range(DEPTH):
        wv = w_mem_kv[l][:, MEM_WIDTH:].reshape(D_MODEL, MEM_HEADS, HEAD_DIM)
        blocks = []
        for hm in range(MEM_HEADS):
            slot = MEM_HEADS - 1 if l < N_A else hm
            blk = jnp.zeros((D_MODEL, MEM_HEADS, HEAD_DIM), F32).at[:, slot].set(wv[:, hm])
            blocks.append(blk.reshape(D_MODEL, GROUP_LANES))
        wv_blocks.append(jnp.concatenate([w_mem_kv[l][:, :MEM_WIDTH]] + blocks, axis=1))
    w_memkv_pad = jnp.stack(wv_blocks).astype(BF16)
    gk_mem = jnp.tile(mem_k_norm, (1, MEM_HEADS)).reshape(DEPTH, 1, MEM_WIDTH)
    mkt, mvw = _mem_kv(mem, mem_norm.reshape(DEPTH, 1, D_MODEL), w_memkv_pad, gk_mem)

    bias = _swa_bias()

    h = x
    kt = v = None
    for l in range(DEPTH):
        g_mix = norm_mix[l].reshape(1, D_MODEL)
        if l < N_A:
            w_in_p, w_out_p = _perm_in_out(w_in[l], w_out[l], _cols_a)
            gq = _cols_a(zeros_main, jnp.tile(mem_q_norm[l], MEM_HEADS)).reshape(1, D_MODEL)
            poolw = jnp.pad(pool_w[l], ((0, 0), (0, GROUP_LANES - POOL_GROUP_DIM),
                                        (0, GROUP_LANES - POOL_GROUP_DIM))).astype(BF16)
            pscale = _cols_a(pool_scale[l], zeros_mem).reshape(1, D_MODEL)
            h = _mix_a(l, h, g_mix, w_in_p, gq, poolw, pscale, mkt, mvw, w_out_p)
        else:
            j = l - N_A
            w_in_p, w_out_p = _perm_in_out(w_in[l], w_out[l], _cols_b)
            gq = _cols_b(jnp.tile(q_norm[j], SWA_Q_HEADS), jnp.tile(mem_q_norm[l], MEM_HEADS))
            h = _mix_b(l, h, sinks[j], g_mix, w_in_p, gq.reshape(1, D_MODEL), kt, v, bias,
                       mkt, mvw, w_out_p)
        mlp_args = (l, h.reshape(B * S, D_MODEL), norm_mlp[l].reshape(1, D_MODEL),
                    w_up[l].astype(BF16), w_down[l].astype(BF16))
        if l == N_A - 1:
            gk = jnp.tile(k_norm, SWA_KV_HEADS).reshape(1, KV_HALF)
            h2d, kt, v = _mlp(*mlp_args, kv_params=(kv_norm.reshape(1, D_MODEL), w_kv.astype(BF16), gk))
        else:
            h2d = _mlp(*mlp_args)
        h = h2d.reshape(B, S, D_MODEL)
    return h
```

```python
import jax
import jax.numpy as jnp
from jax import lax
from jax.experimental import pallas as pl
from jax.experimental.pallas import tpu as pltpu

F32 = jnp.float32
BF16 = jnp.bfloat16

D_MODEL = 1024
DEPTH = 4
N_A = DEPTH // 2
HEAD_DIM = 64
MEM_LEN = 256
MEM_HEADS = 4
MEM_WIDTH = MEM_HEADS * HEAD_DIM
MAIN_WIDTH = D_MODEL - MEM_WIDTH
POOL_WINDOWS = (2, 4, 8, 16)
POOL_GROUPS = len(POOL_WINDOWS)
POOL_GROUP_DIM = MAIN_WIDTH // POOL_GROUPS
SWA_Q_HEADS = MAIN_WIDTH // HEAD_DIM
SWA_KV_HEADS = 4
SWA_GROUP = SWA_Q_HEADS // SWA_KV_HEADS
KV_HALF = SWA_KV_HEADS * HEAD_DIM
BLOCK = 128
D_FF = 4 * D_MODEL
EPS = 1e-6
SCALE = HEAD_DIM ** -0.5
NEG = -1e30

LANES = 128
GROUP_LANES = 256
N_GROUPS = D_MODEL // GROUP_LANES
POOL_HALO = 16

TM_MIX = 1024
SUB_MIX = 256
TM_MLP = 512
KV_CHUNK = 256
VMEM_LIMIT = 60 * 1024 * 1024


def _dot(a, b):
    return jnp.dot(a, b, preferred_element_type=F32)


def _rms(x, g):
    ms = jnp.mean(x * x, axis=-1, keepdims=True)
    return x * lax.rsqrt(ms + EPS) * g


def _half_rms_factor(sq, upper):
    lane = lax.broadcasted_iota(jnp.int32, (1, LANES), 1)
    keep = (lane >= HEAD_DIM) if upper else (lane < HEAD_DIM)
    ss = jnp.sum(jnp.where(keep, sq, 0.0), axis=-1, keepdims=True)
    return lax.rsqrt(ss * (1.0 / HEAD_DIM) + EPS)


def _half_rscale(sq, upper):
    return _half_rms_factor(sq, upper) * SCALE


def _head_norm(x, gain):
    lower = lax.broadcasted_iota(jnp.int32, (1, LANES), 1) < HEAD_DIM
    cols = []
    for c in range(x.shape[1] // LANES):
        lanes = slice(c * LANES, (c + 1) * LANES)
        xc = x[:, lanes]
        sq = xc * xc
        r = jnp.where(lower, _half_rms_factor(sq, upper=False), _half_rms_factor(sq, upper=True))
        cols.append(xc * r * gain[:, lanes])
    return jnp.concatenate(cols, axis=1)


WOUT_ROWS_A = (
    [(g * POOL_GROUP_DIM, POOL_GROUP_DIM, g * GROUP_LANES) for g in range(POOL_GROUPS)]
    + [(MAIN_WIDTH + g * HEAD_DIM, HEAD_DIM, (g + 1) * GROUP_LANES - HEAD_DIM) for g in range(MEM_HEADS)])
WOUT_ROWS_B = (
    [((kvh * SWA_GROUP + g) * HEAD_DIM, HEAD_DIM, g * GROUP_LANES + kvh * HEAD_DIM)
     for kvh in range(SWA_KV_HEADS) for g in range(SWA_GROUP)]
    + [(MAIN_WIDTH, MEM_WIDTH, MAIN_WIDTH)])


def _permute_rows_bf16(dst_ref, src_ref, chunks):
    for src0, n, dst0 in chunks:
        dst_ref[dst0:dst0 + n] = src_ref[src0:src0 + n].astype(BF16)


def _softmax_rows(s):
    m = jnp.max(s, axis=-1, keepdims=True)
    e = jnp.exp(s - m)
    return e * (1.0 / jnp.sum(e, axis=-1, keepdims=True))


def _rows_only(a, r0, r1, dst0=None, n=None):
    n = a.shape[0] if n is None else n
    dst0 = r0 if dst0 is None else dst0
    pieces = []
    if dst0 > 0:
        pieces.append(jnp.zeros((dst0, a.shape[1]), a.dtype))
    pieces.append(a[r0:r1])
    rest = n - dst0 - (r1 - r0)
    if rest > 0:
        pieces.append(jnp.zeros((rest, a.shape[1]), a.dtype))
    return jnp.concatenate(pieces, axis=0)


def _mem_kv_kernel(mem_ref, g_ref, w_ref, gk_ref, mkt_ref, mvw_ref):
    xn = _rms(mem_ref[0], g_ref[0]).astype(BF16)
    kv = _dot(xn, w_ref[0])
    kn = _head_norm(kv[:, :MEM_WIDTH], gk_ref[0])
    mkt_ref[0, 0] = kn.T.astype(BF16)
    mvw_ref[0, 0] = kv[:, MEM_WIDTH:].astype(BF16)


def _mem_kv(mem, mem_norm, w_memkv_pad, gk_mem):
    B = mem.shape[0]
    wide = w_memkv_pad.shape[-1]
    return pl.pallas_call(
        _mem_kv_kernel,
        grid=(DEPTH, B),
        in_specs=[
            pl.BlockSpec((1, MEM_LEN, D_MODEL), lambda l, b: (b, 0, 0)),
            pl.BlockSpec((1, 1, D_MODEL), lambda l, b: (l, 0, 0)),
            pl.BlockSpec((1, D_MODEL, wide), lambda l, b: (l, 0, 0)),
            pl.BlockSpec((1, 1, MEM_WIDTH), lambda l, b: (l, 0, 0)),
        ],
        out_specs=[
            pl.BlockSpec((1, 1, MEM_WIDTH, MEM_LEN), lambda l, b: (l, b, 0, 0)),
            pl.BlockSpec((1, 1, MEM_LEN, wide - MEM_WIDTH), lambda l, b: (l, b, 0, 0)),
        ],
        out_shape=[
            jax.ShapeDtypeStruct((DEPTH, B, MEM_WIDTH, MEM_LEN), BF16),
            jax.ShapeDtypeStruct((DEPTH, B, MEM_LEN, wide - MEM_WIDTH), BF16),
        ],
        compiler_params=pltpu.CompilerParams(
            dimension_semantics=("arbitrary", "arbitrary"), vmem_limit_bytes=VMEM_LIMIT),
        name="mem_kv",
    )(mem, mem_norm, w_memkv_pad, gk_mem)


def _mix_a_kernel(h_ref, g_ref, win_ref, gq_ref, poolw_ref, pscale_ref, mkt_ref, mvw_ref, wout_ref,
                  o_ref, halo_ref, ext_ref, cat_ref, woutp_ref):
    i = pl.program_id(1)
    tm = h_ref.shape[1]

    @pl.when(jnp.logical_and(pl.program_id(0) == 0, i == 0))
    def _():
        _permute_rows_bf16(woutp_ref, wout_ref, WOUT_ROWS_A)

    @pl.when(i == 0)
    def _():
        halo_ref[...] = jnp.zeros_like(halo_ref)

    xn = _rms(h_ref[0], g_ref[...]).astype(BF16)
    proj = _dot(xn, win_ref[...])
    ext_ref[0:POOL_HALO] = halo_ref[...]
    ext_ref[POOL_HALO:] = proj
    halo_ref[...] = proj[tm - POOL_HALO:]

    mkt = mkt_ref[0, 0]
    t1 = (lax.broadcasted_iota(jnp.int32, (POOL_HALO, GROUP_LANES), 0) + (i * tm + 1)).astype(F32)
    for g in range(N_GROUPS):
        lanes = slice(g * GROUP_LANES, (g + 1) * GROUP_LANES)
        s = ext_ref[:, lanes]
        for k in range(g + 1):
            s = s + pltpu.roll(s, 1 << k, 0)
        s = s[POOL_HALO:]
        win = float(POOL_WINDOWS[g])
        pooled = jnp.concatenate(
            [s[:POOL_HALO] / jnp.minimum(t1, win), s[POOL_HALO:] * (1.0 / win)], axis=0)
        d = (pooled - ext_ref[POOL_HALO:, lanes]).astype(BF16)
        mixed = _dot(d, poolw_ref[g]) * pscale_ref[:, lanes]

        mlanes = slice((g + 1) * GROUP_LANES - LANES, (g + 1) * GROUP_LANES)
        qc = ext_ref[POOL_HALO:, mlanes]
        r = _half_rscale(qc * qc, upper=True)
        qn = (qc * gq_ref[:, mlanes]).astype(BF16)
        ktm = _rows_only(mkt, g * HEAD_DIM, (g + 1) * HEAD_DIM, dst0=LANES - HEAD_DIM, n=LANES)
        p = _softmax_rows(_dot(qn, ktm) * r).astype(BF16)
        memo = _dot(p, mvw_ref[0, 0, :, lanes])
        cat_ref[:, lanes] = (mixed + memo).astype(BF16)
    o_ref[0] = h_ref[0] + _dot(cat_ref[...], woutp_ref[...])


def _mix_a(layer, h, g, w_in, gq, poolw, pscale, mkt, mvw, w_out):
    B, S, _ = h.shape
    tm = TM_MIX
    const2 = lambda b, i: (0, 0)
    return pl.pallas_call(
        _mix_a_kernel,
        grid=(B, S // tm),
        in_specs=[
            pl.BlockSpec((1, tm, D_MODEL), lambda b, i: (b, i, 0)),
            pl.BlockSpec((1, D_MODEL), const2),
            pl.BlockSpec((D_MODEL, D_MODEL), const2),
            pl.BlockSpec((1, D_MODEL), const2),
            pl.BlockSpec((N_GROUPS, GROUP_LANES, GROUP_LANES), lambda b, i: (0, 0, 0)),
            pl.BlockSpec((1, D_MODEL), const2),
            pl.BlockSpec((1, 1, MEM_WIDTH, MEM_LEN), lambda b, i: (layer, b, 0, 0)),
            pl.BlockSpec((1, 1, MEM_LEN, D_MODEL), lambda b, i: (layer, b, 0, 0)),
            pl.BlockSpec((D_MODEL, D_MODEL), const2, pipeline_mode=pl.Buffered(1)),
        ],
        out_specs=pl.BlockSpec((1, tm, D_MODEL), lambda b, i: (b, i, 0)),
        out_shape=jax.ShapeDtypeStruct(h.shape, F32),
        scratch_shapes=[
            pltpu.VMEM((POOL_HALO, D_MODEL), F32),
            pltpu.VMEM((tm + POOL_HALO, D_MODEL), F32),
            pltpu.VMEM((tm, D_MODEL), BF16),
            pltpu.VMEM((D_MODEL, D_MODEL), BF16),
        ],
        compiler_params=pltpu.CompilerParams(
            dimension_semantics=("arbitrary", "arbitrary"), vmem_limit_bytes=VMEM_LIMIT),
        name=f"mix_pool_{layer}",
    )(h, g, w_in, gq, poolw, pscale, mkt, mvw, w_out)


def _mix_b_kernel(sinks_ref, h_ref, g_ref, win_ref, gq_ref, ktc_ref, ktp_ref, vc_ref, vp_ref,
                  bias_ref, mkt_ref, mvw_ref, wout_ref, o_ref, xn_ref, proj_ref, cat_ref, woutp_ref):
    i = pl.program_id(1)
    tm = h_ref.shape[1]
    nq = tm // BLOCK

    @pl.when(jnp.logical_and(pl.program_id(0) == 0, i == 0))
    def _():
        _permute_rows_bf16(woutp_ref, wout_ref, WOUT_ROWS_B)

    tri = (lax.broadcasted_iota(jnp.int32, (BLOCK, BLOCK), 0)
           >= lax.broadcasted_iota(jnp.int32, (BLOCK, BLOCK), 1))
    pen = jnp.where(tri, 0.0, jnp.where(i == 0, NEG, 0.0))

    lane_head = lax.broadcasted_iota(jnp.int32, (BLOCK, KV_HALF), 1) // HEAD_DIM
    vblocks = [vp_ref[...]] + [vc_ref[b * BLOCK:(b + 1) * BLOCK] for b in range(nq)]
    vmask = [[jnp.where(lane_head == kvh, vb, jnp.zeros_like(vb)) for kvh in range(SWA_KV_HEADS)]
             for vb in vblocks]

    mkt = mkt_ref[0, 0]
    mvst = jnp.concatenate(
        [mvw_ref[0, 0, :, hm * GROUP_LANES:(hm + 1) * GROUP_LANES] for hm in range(MEM_HEADS)], axis=0)
    mem_lanes = slice(SWA_GROUP * GROUP_LANES, D_MODEL)

    def win_piece(sb, c):
        srows = slice(sb * SUB_MIX, (sb + 1) * SUB_MIX)
        lanes = slice(c * GROUP_LANES, (c + 1) * GROUP_LANES)
        if c == 0:
            xn_ref[srows] = _rms(h_ref[0, srows], g_ref[...]).astype(BF16)
        proj_ref[srows, lanes] = _dot(xn_ref[srows], win_ref[:, lanes])

    def wout_piece(sb, c):
        srows = slice(sb * SUB_MIX, (sb + 1) * SUB_MIX)
        lanes = slice(c * GROUP_LANES, (c + 1) * GROUP_LANES)
        o_ref[0, srows, lanes] = h_ref[0, srows, lanes] + _dot(cat_ref[srows], woutp_ref[:, lanes])

    def swa_block(qb):
        r0 = qb * BLOCK
        rows = slice(r0, r0 + BLOCK)
        if qb == 0:
            ktw = jnp.concatenate([ktp_ref[...], ktc_ref[:, 0:BLOCK]], axis=1)
        else:
            ktw = ktc_ref[:, r0 - BLOCK:r0 + BLOCK]
        pblk = proj_ref[rows, 0:MAIN_WIDTH]
        rcol = []
        for c in range(MAIN_WIDTH // LANES):
            pc = pblk[:, c * LANES:(c + 1) * LANES]
            sq = pc * pc
            rcol.append(_half_rscale(sq, upper=False))
            rcol.append(_half_rscale(sq, upper=True))
        qn = (pblk * gq_ref[:, 0:MAIN_WIDTH]).astype(BF16)
        qst = jnp.concatenate(
            [qn[:, g * GROUP_LANES:(g + 1) * GROUP_LANES] for g in range(SWA_GROUP)], axis=0)
        p_all = []
        v_all = []
        for kvh in range(SWA_KV_HEADS):
            ktm = _rows_only(ktw, kvh * HEAD_DIM, (kvh + 1) * HEAD_DIM)
            s_all = _dot(qst, ktm)
            ps = []
            for g in range(SWA_GROUP):
                hq = kvh * SWA_GROUP + g
                sg = s_all[g * BLOCK:(g + 1) * BLOCK]
                s = (jnp.where(tri, sg[:, BLOCK:], sg[:, :BLOCK]) * rcol[g * SWA_KV_HEADS + kvh]
                     + bias_ref[hq])
                if qb == 0:
                    s = s + pen
                sink = sinks_ref[hq]
                m = jnp.maximum(jnp.max(s, axis=-1, keepdims=True), sink)
                e = jnp.exp(s - m)
                den = jnp.sum(e, axis=-1, keepdims=True) + jnp.exp(sink - m)
                p = e * (1.0 / den)
                ps.append(jnp.concatenate(
                    [jnp.where(tri, 0.0, p).astype(BF16), jnp.where(tri, p, 0.0).astype(BF16)], axis=1))
            p_all.append(jnp.concatenate(ps, axis=0))
            v_all.append(jnp.concatenate([vmask[qb][kvh], vmask[qb + 1][kvh]], axis=0))
        o = _dot(jnp.concatenate(p_all, axis=1), jnp.concatenate(v_all, axis=0))
        for g in range(SWA_GROUP):
            cat_ref[rows, g * GROUP_LANES:(g + 1) * GROUP_LANES] = o[g * BLOCK:(g + 1) * BLOCK].astype(BF16)

    def mem_block(sb):
        srows = slice(sb * SUB_MIX, (sb + 1) * SUB_MIX)
        pmem = proj_ref[srows, mem_lanes]
        qm = (pmem * gq_ref[:, mem_lanes]).astype(BF16)
        pm = []
        for hm in range(MEM_HEADS):
            pc = pmem[:, (hm // 2) * LANES:(hm // 2 + 1) * LANES]
            r = _half_rscale(pc * pc, upper=bool(hm % 2))
            ktm = _rows_only(mkt, hm * HEAD_DIM, (hm + 1) * HEAD_DIM)
            pm.append(_softmax_rows(_dot(qm, ktm) * r).astype(BF16))
        cat_ref[srows, mem_lanes] = _dot(jnp.concatenate(pm, axis=1), mvst).astype(BF16)

    nsub = tm // SUB_MIX
    qps = SUB_MIX // BLOCK
    for c in range(N_GROUPS):
        win_piece(0, c)
    for sb in range(nsub):
        main = [(swa_block, qb) for qb in range(sb * qps, (sb + 1) * qps)] + [(mem_block, sb)]
        fill = []
        if sb + 1 < nsub:
            fill += [(win_piece, sb + 1, c) for c in range(N_GROUPS)]
        if sb >= 1:
            fill += [(wout_piece, sb - 1, c) for c in range(N_GROUPS)]
        per = -(-len(fill) // len(main))
        for k, item in enumerate(main):
            for f in fill[k * per:(k + 1) * per]:
                f[0](*f[1:])
            item[0](*item[1:])
    for c in range(N_GROUPS):
        wout_piece(nsub - 1, c)


def _mix_b(layer, h, sinks, g, w_in, gq, kt, v, bias, mkt, mvw, w_out):
    B, S, _ = h.shape
    tm = TM_MIX
    nq = tm // BLOCK
    nt = S // tm
    const2 = lambda b, i: (0, 0)
    return pl.pallas_call(
        _mix_b_kernel,
        grid=(B, S // tm),
        in_specs=[
            pl.BlockSpec(memory_space=pltpu.SMEM),
            pl.BlockSpec((1, tm, D_MODEL), lambda b, i: (b, i, 0)),
            pl.BlockSpec((1, D_MODEL), const2),
            pl.BlockSpec((D_MODEL, D_MODEL), const2),
            pl.BlockSpec((1, D_MODEL), const2),
            pl.BlockSpec((KV_HALF, tm), lambda b, i: (0, b * nt + i)),
            pl.BlockSpec((KV_HALF, BLOCK), lambda b, i: (0, b * nt * nq + jnp.maximum(i * nq - 1, 0))),
            pl.BlockSpec((tm, KV_HALF), lambda b, i: (b * nt + i, 0)),
            pl.BlockSpec((BLOCK, KV_HALF), lambda b, i: (b * nt * nq + jnp.maximum(i * nq - 1, 0), 0)),
            pl.BlockSpec((SWA_Q_HEADS, BLOCK, BLOCK), lambda b, i: (0, 0, 0)),
            pl.BlockSpec((1, 1, MEM_WIDTH, MEM_LEN), lambda b, i: (layer, b, 0, 0)),
            pl.BlockSpec((1, 1, MEM_LEN, D_MODEL), lambda b, i: (layer, b, 0, 0)),
            pl.BlockSpec((D_MODEL, D_MODEL), const2, pipeline_mode=pl.Buffered(1)),
        ],
        out_specs=pl.BlockSpec((1, tm, D_MODEL), lambda b, i: (b, i, 0)),
        out_shape=jax.ShapeDtypeStruct(h.shape, F32),
        scratch_shapes=[pltpu.VMEM((tm, D_MODEL), BF16), pltpu.VMEM((tm, D_MODEL), F32),
                        pltpu.VMEM((tm, D_MODEL), BF16), pltpu.VMEM((D_MODEL, D_MODEL), BF16)],
        compiler_params=pltpu.CompilerParams(
            dimension_semantics=("arbitrary", "arbitrary"), vmem_limit_bytes=VMEM_LIMIT),
        name=f"mix_swa_{layer}",
    )(sinks, h, g, w_in, gq, kt, kt, v, v, bias, mkt, mvw, w_out)


def _mlp_tile(h_ref, g_ref, wup_ref, wdn_ref):
    x = h_ref[...]
    xn = _rms(x, g_ref[...]).astype(BF16)
    u = jnp.maximum(_dot(xn, wup_ref[...].astype(BF16)), 0.0)
    return x + _dot((u * u).astype(BF16), wdn_ref[...].astype(BF16))


def _mlp_kernel(h_ref, g_ref, wup_ref, wdn_ref, o_ref):
    o_ref[...] = _mlp_tile(h_ref, g_ref, wup_ref, wdn_ref)


def _mlp_kv_kernel(h_ref, g_ref, wup_ref, wdn_ref, gkv_ref, wkv_ref, gk_ref, o_ref, kt_ref, v_ref):
    out = _mlp_tile(h_ref, g_ref, wup_ref, wdn_ref)
    o_ref[...] = out
    for r0 in range(0, out.shape[0], KV_CHUNK):
        rows = slice(r0, r0 + KV_CHUNK)
        kv = _dot(_rms(out[rows], gkv_ref[...]).astype(BF16), wkv_ref[...].astype(BF16))
        kn = _head_norm(kv[:, :KV_HALF], gk_ref[...])
        kt_ref[:, rows] = kn.T.astype(BF16)
        v_ref[rows] = kv[:, KV_HALF:].astype(BF16)


def _mlp(layer, h2d, g, w_up, w_down, kv_params=None):
    T = h2d.shape[0]
    tm = TM_MLP
    const = lambda i: (0, 0)
    in_specs = [
        pl.BlockSpec((tm, D_MODEL), lambda i: (i, 0)),
        pl.BlockSpec((1, D_MODEL), const),
        pl.BlockSpec((D_MODEL, D_FF), const, pipeline_mode=pl.Buffered(1)),
        pl.BlockSpec((D_FF, D_MODEL), const, pipeline_mode=pl.Buffered(1)),
    ]
    out_specs = pl.BlockSpec((tm, D_MODEL), lambda i: (i, 0))
    out_shape = jax.ShapeDtypeStruct(h2d.shape, F32)
    args = (h2d, g, w_up, w_down)
    body = _mlp_kernel
    if kv_params is not None:
        in_specs += [
            pl.BlockSpec((1, D_MODEL), const),
            pl.BlockSpec((D_MODEL, 2 * KV_HALF), const, pipeline_mode=pl.Buffered(1)),
            pl.BlockSpec((1, KV_HALF), const),
        ]
        out_specs = [out_specs, pl.BlockSpec((KV_HALF, tm), lambda i: (0, i)),
                     pl.BlockSpec((tm, KV_HALF), lambda i: (i, 0))]
        out_shape = [out_shape, jax.ShapeDtypeStruct((KV_HALF, T), BF16),
                     jax.ShapeDtypeStruct((T, KV_HALF), BF16)]
        args += tuple(kv_params)
        body = _mlp_kv_kernel
    return pl.pallas_call(
        body,
        grid=(T // tm,),
        in_specs=in_specs,
        out_specs=out_specs,
        out_shape=out_shape,
        compiler_params=pltpu.CompilerParams(
            dimension_semantics=("arbitrary",), vmem_limit_bytes=VMEM_LIMIT),
        name=f"mlp_{layer}",
    )(*args)


def _cols_a(main, memq):
    lead = main.shape[:-1]
    m = main.reshape(*lead, POOL_GROUPS, POOL_GROUP_DIM)
    q = memq.reshape(*lead, MEM_HEADS, HEAD_DIM)
    return jnp.concatenate([m, q], axis=-1).reshape(*lead, D_MODEL)


def _cols_b(main, memq):
    lead = main.shape[:-1]
    m = main.reshape(*lead, SWA_KV_HEADS, SWA_GROUP, HEAD_DIM)
    m = jnp.swapaxes(m, -3, -2).reshape(*lead, MAIN_WIDTH)
    return jnp.concatenate([m, memq], axis=-1)


def _perm_in(w_in_l, cols):
    return cols(w_in_l[:, :MAIN_WIDTH], w_in_l[:, MAIN_WIDTH:]).astype(BF16)


def _swa_bias():
    qi = jnp.arange(BLOCK, dtype=jnp.int32)[:, None]
    kj = jnp.arange(BLOCK, dtype=jnp.int32)[None, :]
    dist = jnp.where(kj <= qi, qi - kj, qi - kj + BLOCK)
    slopes = jnp.exp2(-8.0 * jnp.arange(1, SWA_Q_HEADS + 1, dtype=F32) / SWA_Q_HEADS)
    return -(slopes[:, None, None] * dist.astype(F32)[None])


def kernel(x, mem, norm_mix, w_in, pool_w, pool_scale, kv_norm, w_kv, k_norm, q_norm, sinks, mem_norm,
           w_mem_kv, mem_q_norm, mem_k_norm, w_out, norm_mlp, w_up, w_down):
    B, S, _ = x.shape
    zeros_main = jnp.zeros((MAIN_WIDTH,), F32)
    zeros_mem = jnp.zeros((MEM_WIDTH,), F32)

    wv = w_mem_kv[:, :, MEM_WIDTH:].reshape(DEPTH, D_MODEL, MEM_HEADS, 1, HEAD_DIM)
    slot = jnp.where(jnp.arange(DEPTH)[:, None] < N_A, MEM_HEADS - 1, jnp.arange(MEM_HEADS)[None, :])
    place = slot[:, None, :, None, None] == jnp.arange(MEM_HEADS)[None, None, None, :, None]
    wv_pad = jnp.where(place, wv, 0.0).reshape(DEPTH, D_MODEL, MEM_HEADS * GROUP_LANES)
    w_memkv_pad = jnp.concatenate([w_mem_kv[:, :, :MEM_WIDTH], wv_pad], axis=-1).astype(BF16)
    gk_mem = jnp.tile(mem_k_norm, (1, MEM_HEADS)).reshape(DEPTH, 1, MEM_WIDTH)
    mkt, mvw = _mem_kv(mem, mem_norm.reshape(DEPTH, 1, D_MODEL), w_memkv_pad, gk_mem)

    bias = _swa_bias()

    h = x
    kt = v = None
    for l in range(DEPTH):
        g_mix = norm_mix[l].reshape(1, D_MODEL)
        if l < N_A:
            w_in_p = _perm_in(w_in[l], _cols_a)
            gq = _cols_a(zeros_main, jnp.tile(mem_q_norm[l], MEM_HEADS)).reshape(1, D_MODEL)
            poolw = jnp.pad(pool_w[l], ((0, 0), (0, GROUP_LANES - POOL_GROUP_DIM),
                                        (0, GROUP_LANES - POOL_GROUP_DIM))).astype(BF16)
            pscale = _cols_a(pool_scale[l], zeros_mem).reshape(1, D_MODEL)
            h = _mix_a(l, h, g_mix, w_in_p, gq, poolw, pscale, mkt, mvw, w_out[l])
        else:
            j = l - N_A
            w_in_p = _perm_in(w_in[l], _cols_b)
            gq = _cols_b(jnp.tile(q_norm[j], SWA_Q_HEADS), jnp.tile(mem_q_norm[l], MEM_HEADS))
            h = _mix_b(l, h, sinks[j], g_mix, w_in_p, gq.reshape(1, D_MODEL), kt, v, bias,
                       mkt, mvw, w_out[l])
        mlp_args = (l, h.reshape(B * S, D_MODEL), norm_mlp[l].reshape(1, D_MODEL),
                    w_up[l], w_down[l])
        if l == N_A - 1:
            gk = jnp.tile(k_norm, SWA_KV_HEADS).reshape(1, KV_HALF)
            h2d, kt, v = _mlp(*mlp_args, kv_params=(kv_norm.reshape(1, D_MODEL), w_kv, gk))
        else:
            h2d = _mlp(*mlp_args)
        h = h2d.reshape(B, S, D_MODEL)
    return h
```

```python
import jax
import jax.numpy as jnp
from jax import lax
from jax.experimental import pallas as pl
from jax.experimental.pallas import tpu as pltpu

F32 = jnp.float32
BF16 = jnp.bfloat16

D_MODEL = 1024
DEPTH = 4
N_A = DEPTH // 2
HEAD_DIM = 64
MEM_LEN = 256
MEM_HEADS = 4
MEM_WIDTH = MEM_HEADS * HEAD_DIM
MAIN_WIDTH = D_MODEL - MEM_WIDTH
POOL_WINDOWS = (2, 4, 8, 16)
POOL_GROUPS = len(POOL_WINDOWS)
POOL_GROUP_DIM = MAIN_WIDTH // POOL_GROUPS
SWA_Q_HEADS = MAIN_WIDTH // HEAD_DIM
SWA_KV_HEADS = 4
SWA_GROUP = SWA_Q_HEADS // SWA_KV_HEADS
KV_HALF = SWA_KV_HEADS * HEAD_DIM
BLOCK = 128
D_FF = 4 * D_MODEL
EPS = 1e-6
SCALE = HEAD_DIM ** -0.5
NEG = -1e30

LANES = 128
GROUP_LANES = 256
N_GROUPS = D_MODEL // GROUP_LANES
POOL_HALO = 16

TM_MIX = 1024
SUB_MIX = 256
TM_MLP = 512
KV_CHUNK = 256
VMEM_LIMIT = 60 * 1024 * 1024


def _dot(a, b):
    return jnp.dot(a, b, preferred_element_type=F32)


def _rms(x, g):
    ms = jnp.mean(x * x, axis=-1, keepdims=True)
    return x * lax.rsqrt(ms + EPS) * g


def _half_rms_factor(sq, upper):
    lane = lax.broadcasted_iota(jnp.int32, (1, LANES), 1)
    keep = (lane >= HEAD_DIM) if upper else (lane < HEAD_DIM)
    ss = jnp.sum(jnp.where(keep, sq, 0.0), axis=-1, keepdims=True)
    return lax.rsqrt(ss * (1.0 / HEAD_DIM) + EPS)


def _half_rscale(sq, upper):
    return _half_rms_factor(sq, upper) * SCALE


def _head_norm(x, gain):
    lower = lax.broadcasted_iota(jnp.int32, (1, LANES), 1) < HEAD_DIM
    cols = []
    for c in range(x.shape[1] // LANES):
        lanes = slice(c * LANES, (c + 1) * LANES)
        xc = x[:, lanes]
        sq = xc * xc
        r = jnp.where(lower, _half_rms_factor(sq, upper=False), _half_rms_factor(sq, upper=True))
        cols.append(xc * r * gain[:, lanes])
    return jnp.concatenate(cols, axis=1)


WOUT_ROWS_A = (
    [(g * POOL_GROUP_DIM, POOL_GROUP_DIM, g * GROUP_LANES) for g in range(POOL_GROUPS)]
    + [(MAIN_WIDTH + g * HEAD_DIM, HEAD_DIM, (g + 1) * GROUP_LANES - HEAD_DIM) for g in range(MEM_HEADS)])
WOUT_ROWS_B = (
    [((kvh * SWA_GROUP + g) * HEAD_DIM, HEAD_DIM, g * GROUP_LANES + kvh * HEAD_DIM)
     for kvh in range(SWA_KV_HEADS) for g in range(SWA_GROUP)]
    + [(MAIN_WIDTH, MEM_WIDTH, MAIN_WIDTH)])


def _permute_rows_bf16(dst_ref, src_ref, chunks):
    for src0, n, dst0 in chunks:
        dst_ref[dst0:dst0 + n] = src_ref[src0:src0 + n].astype(BF16)


def _softmax_rows(s):
    m = jnp.max(s, axis=-1, keepdims=True)
    e = jnp.exp(s - m)
    return e * (1.0 / jnp.sum(e, axis=-1, keepdims=True))


def _rows_only(a, r0, r1, dst0=None, n=None):
    n = a.shape[0] if n is None else n
    dst0 = r0 if dst0 is None else dst0
    pieces = []
    if dst0 > 0:
        pieces.append(jnp.zeros((dst0, a.shape[1]), a.dtype))
    pieces.append(a[r0:r1])
    rest = n - dst0 - (r1 - r0)
    if rest > 0:
        pieces.append(jnp.zeros((rest, a.shape[1]), a.dtype))
    return jnp.concatenate(pieces, axis=0)


def _mem_kv_kernel(mem_ref, g_ref, w_ref, gk_ref, mkt_ref, mvw_ref):
    m = mem_ref[0]
    lane_slot = lax.broadcasted_iota(jnp.int32, (1, MEM_WIDTH), 1) // HEAD_DIM
    last = MEM_HEADS - 1
    for l in range(DEPTH):
        xn = _rms(m, g_ref[l]).astype(BF16)
        kv = _dot(xn, w_ref[l].astype(BF16))
        mkt_ref[l, 0] = _head_norm(kv[:, :MEM_WIDTH], gk_ref[l]).T.astype(BF16)
        v = kv[:, MEM_WIDTH:]
        for hm in range(MEM_HEADS):
            if l < N_A:
                moved = v if hm == last else pltpu.roll(v, (last - hm) * HEAD_DIM, 1)
                blk = jnp.where(lane_slot == last, moved, 0.0)
            else:
                blk = jnp.where(lane_slot == hm, v, 0.0)
            mvw_ref[l, 0, :, hm * GROUP_LANES:(hm + 1) * GROUP_LANES] = blk.astype(BF16)


def _mem_kv(mem, mem_norm, w_mem_kv, gk_mem):
    B = mem.shape[0]
    const3 = lambda b: (0, 0, 0)
    return pl.pallas_call(
        _mem_kv_kernel,
        grid=(B,),
        in_specs=[
            pl.BlockSpec((1, MEM_LEN, D_MODEL), lambda b: (b, 0, 0)),
            pl.BlockSpec((DEPTH, 1, D_MODEL), const3),
            pl.BlockSpec((DEPTH, D_MODEL, 2 * MEM_WIDTH), const3, pipeline_mode=pl.Buffered(1)),
            pl.BlockSpec((DEPTH, 1, MEM_WIDTH), const3),
        ],
        out_specs=[
            pl.BlockSpec((DEPTH, 1, MEM_WIDTH, MEM_LEN), lambda b: (0, b, 0, 0)),
            pl.BlockSpec((DEPTH, 1, MEM_LEN, D_MODEL), lambda b: (0, b, 0, 0)),
        ],
        out_shape=[
            jax.ShapeDtypeStruct((DEPTH, B, MEM_WIDTH, MEM_LEN), BF16),
            jax.ShapeDtypeStruct((DEPTH, B, MEM_LEN, D_MODEL), BF16),
        ],
        compiler_params=pltpu.CompilerParams(
            dimension_semantics=("arbitrary",), vmem_limit_bytes=VMEM_LIMIT),
        name="mem_kv",
    )(mem, mem_norm, w_mem_kv, gk_mem)


def _mix_a_kernel(h_ref, g_ref, win_ref, gq_ref, poolw_ref, pscale_ref, mkt_ref, mvw_ref, wout_ref,
                  o_ref, halo_ref, ext_ref, cat_ref, woutp_ref):
    i = pl.program_id(1)
    tm = h_ref.shape[1]

    @pl.when(jnp.logical_and(pl.program_id(0) == 0, i == 0))
    def _():
        _permute_rows_bf16(woutp_ref, wout_ref, WOUT_ROWS_A)

    @pl.when(i == 0)
    def _():
        halo_ref[...] = jnp.zeros_like(halo_ref)

    xn = _rms(h_ref[0], g_ref[...]).astype(BF16)
    proj = _dot(xn, win_ref[...])
    ext_ref[0:POOL_HALO] = halo_ref[...]
    ext_ref[POOL_HALO:] = proj
    halo_ref[...] = proj[tm - POOL_HALO:]

    mkt = mkt_ref[0, 0]
    t1 = (lax.broadcasted_iota(jnp.int32, (POOL_HALO, GROUP_LANES), 0) + (i * tm + 1)).astype(F32)
    for g in range(N_GROUPS):
        lanes = slice(g * GROUP_LANES, (g + 1) * GROUP_LANES)
        s = ext_ref[:, lanes]
        for k in range(g + 1):
            s = s + pltpu.roll(s, 1 << k, 0)
        s = s[POOL_HALO:]
        win = float(POOL_WINDOWS[g])
        pooled = jnp.concatenate(
            [s[:POOL_HALO] / jnp.minimum(t1, win), s[POOL_HALO:] * (1.0 / win)], axis=0)
        d = (pooled - ext_ref[POOL_HALO:, lanes]).astype(BF16)
        mixed = _dot(d, poolw_ref[g]) * pscale_ref[:, lanes]

        mlanes = slice((g + 1) * GROUP_LANES - LANES, (g + 1) * GROUP_LANES)
        qc = ext_ref[POOL_HALO:, mlanes]
        r = _half_rscale(qc * qc, upper=True)
        qn = (qc * gq_ref[:, mlanes]).astype(BF16)
        ktm = _rows_only(mkt, g * HEAD_DIM, (g + 1) * HEAD_DIM, dst0=LANES - HEAD_DIM, n=LANES)
        p = _softmax_rows(_dot(qn, ktm) * r).astype(BF16)
        memo = _dot(p, mvw_ref[0, 0, :, lanes])
        cat_ref[:, lanes] = (mixed + memo).astype(BF16)
    o_ref[0] = h_ref[0] + _dot(cat_ref[...], woutp_ref[...])


def _mix_a(layer, h, g, w_in, gq, poolw, pscale, mkt, mvw, w_out):
    B, S, _ = h.shape
    tm = TM_MIX
    const2 = lambda b, i: (0, 0)
    return pl.pallas_call(
        _mix_a_kernel,
        grid=(B, S // tm),
        in_specs=[
            pl.BlockSpec((1, tm, D_MODEL), lambda b, i: (b, i, 0)),
            pl.BlockSpec((1, D_MODEL), const2),
            pl.BlockSpec((D_MODEL, D_MODEL), const2),
            pl.BlockSpec((1, D_MODEL), const2),
            pl.BlockSpec((N_GROUPS, GROUP_LANES, GROUP_LANES), lambda b, i: (0, 0, 0)),
            pl.BlockSpec((1, D_MODEL), const2),
            pl.BlockSpec((1, 1, MEM_WIDTH, MEM_LEN), lambda b, i: (layer, b, 0, 0)),
            pl.BlockSpec((1, 1, MEM_LEN, D_MODEL), lambda b, i: (layer, b, 0, 0)),
            pl.BlockSpec((None, D_MODEL, D_MODEL), lambda b, i: (layer, 0, 0), pipeline_mode=pl.Buffered(1)),
        ],
        out_specs=pl.BlockSpec((1, tm, D_MODEL), lambda b, i: (b, i, 0)),
        out_shape=jax.ShapeDtypeStruct(h.shape, F32),
        scratch_shapes=[
            pltpu.VMEM((POOL_HALO, D_MODEL), F32),
            pltpu.VMEM((tm + POOL_HALO, D_MODEL), F32),
            pltpu.VMEM((tm, D_MODEL), BF16),
            pltpu.VMEM((D_MODEL, D_MODEL), BF16),
        ],
        compiler_params=pltpu.CompilerParams(
            dimension_semantics=("arbitrary", "arbitrary"), vmem_limit_bytes=VMEM_LIMIT),
        name=f"mix_pool_{layer}",
    )(h, g, w_in, gq, poolw, pscale, mkt, mvw, w_out)


def _mix_b_kernel(sinks_ref, h_ref, g_ref, win_ref, gq_ref, ktc_ref, ktp_ref, vc_ref, vp_ref,
                  bias_ref, mkt_ref, mvw_ref, wout_ref, o_ref, xn_ref, proj_ref, cat_ref, woutp_ref):
    i = pl.program_id(1)
    tm = h_ref.shape[1]
    nq = tm // BLOCK

    @pl.when(jnp.logical_and(pl.program_id(0) == 0, i == 0))
    def _():
        _permute_rows_bf16(woutp_ref, wout_ref, WOUT_ROWS_B)

    tri = (lax.broadcasted_iota(jnp.int32, (BLOCK, BLOCK), 0)
           >= lax.broadcasted_iota(jnp.int32, (BLOCK, BLOCK), 1))
    pen = jnp.where(tri, 0.0, jnp.where(i == 0, NEG, 0.0))

    lane_head = lax.broadcasted_iota(jnp.int32, (BLOCK, KV_HALF), 1) // HEAD_DIM
    vblocks = [vp_ref[...]] + [vc_ref[b * BLOCK:(b + 1) * BLOCK] for b in range(nq)]
    vmask = [[jnp.where(lane_head == kvh, vb, jnp.zeros_like(vb)) for kvh in range(SWA_KV_HEADS)]
             for vb in vblocks]

    mkt = mkt_ref[0, 0]
    mvst = jnp.concatenate(
        [mvw_ref[0, 0, :, hm * GROUP_LANES:(hm + 1) * GROUP_LANES] for hm in range(MEM_HEADS)], axis=0)
    mem_lanes = slice(SWA_GROUP * GROUP_LANES, D_MODEL)

    def win_piece(sb, c):
        srows = slice(sb * SUB_MIX, (sb + 1) * SUB_MIX)
        lanes = slice(c * GROUP_LANES, (c + 1) * GROUP_LANES)
        if c == 0:
            xn_ref[srows] = _rms(h_ref[0, srows], g_ref[...]).astype(BF16)
        proj_ref[srows, lanes] = _dot(xn_ref[srows], win_ref[:, lanes])

    def wout_piece(sb, c):
        srows = slice(sb * SUB_MIX, (sb + 1) * SUB_MIX)
        lanes = slice(c * GROUP_LANES, (c + 1) * GROUP_LANES)
        o_ref[0, srows, lanes] = h_ref[0, srows, lanes] + _dot(cat_ref[srows], woutp_ref[:, lanes])

    def swa_block(qb):
        r0 = qb * BLOCK
        rows = slice(r0, r0 + BLOCK)
        if qb == 0:
            ktw = jnp.concatenate([ktp_ref[...], ktc_ref[:, 0:BLOCK]], axis=1)
        else:
            ktw = ktc_ref[:, r0 - BLOCK:r0 + BLOCK]
        pblk = proj_ref[rows, 0:MAIN_WIDTH]
        rcol = []
        for c in range(MAIN_WIDTH // LANES):
            pc = pblk[:, c * LANES:(c + 1) * LANES]
            sq = pc * pc
            rcol.append(_half_rscale(sq, upper=False))
            rcol.append(_half_rscale(sq, upper=True))
        qn = (pblk * gq_ref[:, 0:MAIN_WIDTH]).astype(BF16)
        qst = jnp.concatenate(
            [qn[:, g * GROUP_LANES:(g + 1) * GROUP_LANES] for g in range(SWA_GROUP)], axis=0)
        p_all = []
        v_all = []
        for kvh in range(SWA_KV_HEADS):
            ktm = _rows_only(ktw, kvh * HEAD_DIM, (kvh + 1) * HEAD_DIM)
            s_all = _dot(qst, ktm)
            ps = []
            for g in range(SWA_GROUP):
                hq = kvh * SWA_GROUP + g
                sg = s_all[g * BLOCK:(g + 1) * BLOCK]
                s = (jnp.where(tri, sg[:, BLOCK:], sg[:, :BLOCK]) * rcol[g * SWA_KV_HEADS + kvh]
                     + bias_ref[hq])
                if qb == 0:
                    s = s + pen
                sink = sinks_ref[hq]
                m = jnp.maximum(jnp.max(s, axis=-1, keepdims=True), sink)
                e = jnp.exp(s - m)
                den = jnp.sum(e, axis=-1, keepdims=True) + jnp.exp(sink - m)
                p = e * (1.0 / den)
                ps.append(jnp.concatenate(
                    [jnp.where(tri, 0.0, p).astype(BF16), jnp.where(tri, p, 0.0).astype(BF16)], axis=1))
            p_all.append(jnp.concatenate(ps, axis=0))
            v_all.append(jnp.concatenate([vmask[qb][kvh], vmask[qb + 1][kvh]], axis=0))
        o = _dot(jnp.concatenate(p_all, axis=1), jnp.concatenate(v_all, axis=0))
        for g in range(SWA_GROUP):
            cat_ref[rows, g * GROUP_LANES:(g + 1) * GROUP_LANES] = o[g * BLOCK:(g + 1) * BLOCK].astype(BF16)

    def mem_block(sb):
        srows = slice(sb * SUB_MIX, (sb + 1) * SUB_MIX)
        pmem = proj_ref[srows, mem_lanes]
        qm = (pmem * gq_ref[:, mem_lanes]).astype(BF16)
        pm = []
        for hm in range(MEM_HEADS):
            pc = pmem[:, (hm // 2) * LANES:(hm // 2 + 1) * LANES]
            r = _half_rscale(pc * pc, upper=bool(hm % 2))
            ktm = _rows_only(mkt, hm * HEAD_DIM, (hm + 1) * HEAD_DIM)
            pm.append(_softmax_rows(_dot(qm, ktm) * r).astype(BF16))
        cat_ref[srows, mem_lanes] = _dot(jnp.concatenate(pm, axis=1), mvst).astype(BF16)

    nsub = tm // SUB_MIX
    qps = SUB_MIX // BLOCK
    for c in range(N_GROUPS):
        win_piece(0, c)
    for sb in range(nsub):
        main = [(swa_block, qb) for qb in range(sb * qps, (sb + 1) * qps)] + [(mem_block, sb)]
        fill = []
        if sb + 1 < nsub:
            fill += [(win_piece, sb + 1, c) for c in range(N_GROUPS)]
        if sb >= 1:
            fill += [(wout_piece, sb - 1, c) for c in range(N_GROUPS)]
        per = -(-len(fill) // len(main))
        for k, item in enumerate(main):
            for f in fill[k * per:(k + 1) * per]:
                f[0](*f[1:])
            item[0](*item[1:])
    for c in range(N_GROUPS):
        wout_piece(nsub - 1, c)


def _mix_b(layer, h, sinks, g, w_in, gq, kt, v, bias, mkt, mvw, w_out):
    B, S, _ = h.shape
    tm = TM_MIX
    nq = tm // BLOCK
    nt = S // tm
    const2 = lambda b, i: (0, 0)
    return pl.pallas_call(
        _mix_b_kernel,
        grid=(B, S // tm),
        in_specs=[
            pl.BlockSpec(memory_space=pltpu.SMEM),
            pl.BlockSpec((1, tm, D_MODEL), lambda b, i: (b, i, 0)),
            pl.BlockSpec((1, D_MODEL), const2),
            pl.BlockSpec((D_MODEL, D_MODEL), const2),
            pl.BlockSpec((1, D_MODEL), const2),
            pl.BlockSpec((KV_HALF, tm), lambda b, i: (0, b * nt + i)),
            pl.BlockSpec((KV_HALF, BLOCK), lambda b, i: (0, b * nt * nq + jnp.maximum(i * nq - 1, 0))),
            pl.BlockSpec((tm, KV_HALF), lambda b, i: (b * nt + i, 0)),
            pl.BlockSpec((BLOCK, KV_HALF), lambda b, i: (b * nt * nq + jnp.maximum(i * nq - 1, 0), 0)),
            pl.BlockSpec((SWA_Q_HEADS, BLOCK, BLOCK), lambda b, i: (0, 0, 0)),
            pl.BlockSpec((1, 1, MEM_WIDTH, MEM_LEN), lambda b, i: (layer, b, 0, 0)),
            pl.BlockSpec((1, 1, MEM_LEN, D_MODEL), lambda b, i: (layer, b, 0, 0)),
            pl.BlockSpec((None, D_MODEL, D_MODEL), lambda b, i: (layer, 0, 0), pipeline_mode=pl.Buffered(1)),
        ],
        out_specs=pl.BlockSpec((1, tm, D_MODEL), lambda b, i: (b, i, 0)),
        out_shape=jax.ShapeDtypeStruct(h.shape, F32),
        scratch_shapes=[pltpu.VMEM((tm, D_MODEL), BF16), pltpu.VMEM((tm, D_MODEL), F32),
                        pltpu.VMEM((tm, D_MODEL), BF16), pltpu.VMEM((D_MODEL, D_MODEL), BF16)],
        compiler_params=pltpu.CompilerParams(
            dimension_semantics=("arbitrary", "arbitrary"), vmem_limit_bytes=VMEM_LIMIT),
        name=f"mix_swa_{layer}",
    )(sinks, h, g, w_in, gq, kt, kt, v, v, bias, mkt, mvw, w_out)


def _mlp_tile(h_ref, g_ref, wup_ref, wdn_ref):
    x = h_ref[...]
    xn = _rms(x, g_ref[...]).astype(BF16)
    u = jnp.maximum(_dot(xn, wup_ref[...].astype(BF16)), 0.0)
    return x + _dot((u * u).astype(BF16), wdn_ref[...].astype(BF16))


def _mlp_kernel(h_ref, g_ref, wup_ref, wdn_ref, o_ref):
    o_ref[...] = _mlp_tile(h_ref, g_ref, wup_ref, wdn_ref)


def _mlp_kv_kernel(h_ref, g_ref, wup_ref, wdn_ref, gkv_ref, wkv_ref, gk_ref, o_ref, kt_ref, v_ref):
    out = _mlp_tile(h_ref, g_ref, wup_ref, wdn_ref)
    o_ref[...] = out
    for r0 in range(0, out.shape[0], KV_CHUNK):
        rows = slice(r0, r0 + KV_CHUNK)
        kv = _dot(_rms(out[rows], gkv_ref[...]).astype(BF16), wkv_ref[...].astype(BF16))
        kn = _head_norm(kv[:, :KV_HALF], gk_ref[...])
        kt_ref[:, rows] = kn.T.astype(BF16)
        v_ref[rows] = kv[:, KV_HALF:].astype(BF16)


def _mlp(layer, h2d, g, w_up, w_down, kv_params=None):
    T = h2d.shape[0]
    tm = TM_MLP
    const = lambda i: (0, 0)
    in_specs = [
        pl.BlockSpec((tm, D_MODEL), lambda i: (i, 0)),
        pl.BlockSpec((1, D_MODEL), const),
        pl.BlockSpec((None, D_MODEL, D_FF), lambda i: (layer, 0, 0), pipeline_mode=pl.Buffered(1)),
        pl.BlockSpec((None, D_FF, D_MODEL), lambda i: (layer, 0, 0), pipeline_mode=pl.Buffered(1)),
    ]
    out_specs = pl.BlockSpec((tm, D_MODEL), lambda i: (i, 0))
    out_shape = jax.ShapeDtypeStruct(h2d.shape, F32)
    args = (h2d, g, w_up, w_down)
    body = _mlp_kernel
    if kv_params is not None:
        in_specs += [
            pl.BlockSpec((1, D_MODEL), const),
            pl.BlockSpec((D_MODEL, 2 * KV_HALF), const, pipeline_mode=pl.Buffered(1)),
            pl.BlockSpec((1, KV_HALF), const),
        ]
        out_specs = [out_specs, pl.BlockSpec((KV_HALF, tm), lambda i: (0, i)),
                     pl.BlockSpec((tm, KV_HALF), lambda i: (i, 0))]
        out_shape = [out_shape, jax.ShapeDtypeStruct((KV_HALF, T), BF16),
                     jax.ShapeDtypeStruct((T, KV_HALF), BF16)]
        args += tuple(kv_params)
        body = _mlp_kv_kernel
    return pl.pallas_call(
        body,
        grid=(T // tm,),
        in_specs=in_specs,
        out_specs=out_specs,
        out_shape=out_shape,
        compiler_params=pltpu.CompilerParams(
            dimension_semantics=("arbitrary",), vmem_limit_bytes=VMEM_LIMIT),
        name=f"mlp_{layer}",
    )(*args)


def _cols_a(main, memq):
    lead = main.shape[:-1]
    m = main.reshape(*lead, POOL_GROUPS, POOL_GROUP_DIM)
    q = memq.reshape(*lead, MEM_HEADS, HEAD_DIM)
    return jnp.concatenate([m, q], axis=-1).reshape(*lead, D_MODEL)


def _cols_b(main, memq):
    lead = main.shape[:-1]
    m = main.reshape(*lead, SWA_KV_HEADS, SWA_GROUP, HEAD_DIM)
    m = jnp.swapaxes(m, -3, -2).reshape(*lead, MAIN_WIDTH)
    return jnp.concatenate([m, memq], axis=-1)


def _perm_in(w_in_l, cols):
    return cols(w_in_l[:, :MAIN_WIDTH], w_in_l[:, MAIN_WIDTH:]).astype(BF16)


def _swa_bias():
    qi = jnp.arange(BLOCK, dtype=jnp.int32)[:, None]
    kj = jnp.arange(BLOCK, dtype=jnp.int32)[None, :]
    dist = jnp.where(kj <= qi, qi - kj, qi - kj + BLOCK)
    slopes = jnp.exp2(-8.0 * jnp.arange(1, SWA_Q_HEADS + 1, dtype=F32) / SWA_Q_HEADS)
    return -(slopes[:, None, None] * dist.astype(F32)[None])


def kernel(x, mem, norm_mix, w_in, pool_w, pool_scale, kv_norm, w_kv, k_norm, q_norm, sinks, mem_norm,
           w_mem_kv, mem_q_norm, mem_k_norm, w_out, norm_mlp, w_up, w_down):
    B, S, _ = x.shape
    zeros_main = jnp.zeros((MAIN_WIDTH,), F32)
    zeros_mem = jnp.zeros((MEM_WIDTH,), F32)

    gk_mem = jnp.tile(mem_k_norm, (1, MEM_HEADS)).reshape(DEPTH, 1, MEM_WIDTH)
    mkt, mvw = _mem_kv(mem, mem_norm.reshape(DEPTH, 1, D_MODEL), w_mem_kv, gk_mem)

    bias = _swa_bias()

    h = x
    kt = v = None
    for l in range(DEPTH):
        g_mix = norm_mix[l].reshape(1, D_MODEL)
        if l < N_A:
            w_in_p = _perm_in(w_in[l], _cols_a)
            gq = _cols_a(zeros_main, jnp.tile(mem_q_norm[l], MEM_HEADS)).reshape(1, D_MODEL)
            poolw = jnp.pad(pool_w[l], ((0, 0), (0, GROUP_LANES - POOL_GROUP_DIM),
                                        (0, GROUP_LANES - POOL_GROUP_DIM))).astype(BF16)
            pscale = _cols_a(pool_scale[l], zeros_mem).reshape(1, D_MODEL)
            h = _mix_a(l, h, g_mix, w_in_p, gq, poolw, pscale, mkt, mvw, w_out)
        else:
            j = l - N_A
            w_in_p = _perm_in(w_in[l], _cols_b)
            gq = _cols_b(jnp.tile(q_norm[j], SWA_Q_HEADS), jnp.tile(mem_q_norm[l], MEM_HEADS))
            h = _mix_b(l, h, sinks[j], g_mix, w_in_p, gq.reshape(1, D_MODEL), kt, v, bias,
                       mkt, mvw, w_out)
        mlp_args = (l, h.reshape(B * S, D_MODEL), norm_mlp[l].reshape(1, D_MODEL),
                    w_up, w_down)
        if l == N_A - 1:
            gk = jnp.tile(k_norm, SWA_KV_HEADS).reshape(1, KV_HALF)
            h2d, kt, v = _mlp(*mlp_args, kv_params=(kv_norm.reshape(1, D_MODEL), w_kv, gk))
        else:
            h2d = _mlp(*mlp_args)
        h = h2d.reshape(B, S, D_MODEL)
    return h
```

```python
import jax
import jax.numpy as jnp
from jax import lax
from jax.experimental import pallas as pl
from jax.experimental.pallas import tpu as pltpu

F32 = jnp.float32
BF16 = jnp.bfloat16

D_MODEL = 1024
DEPTH = 4
N_A = DEPTH // 2
HEAD_DIM = 64
MEM_LEN = 256
MEM_HEADS = 4
MEM_WIDTH = MEM_HEADS * HEAD_DIM
MAIN_WIDTH = D_MODEL - MEM_WIDTH
POOL_WINDOWS = (2, 4, 8, 16)
POOL_GROUPS = len(POOL_WINDOWS)
POOL_GROUP_DIM = MAIN_WIDTH // POOL_GROUPS
SWA_Q_HEADS = MAIN_WIDTH // HEAD_DIM
SWA_KV_HEADS = 4
SWA_GROUP = SWA_Q_HEADS // SWA_KV_HEADS
KV_HALF = SWA_KV_HEADS * HEAD_DIM
BLOCK = 128
D_FF = 4 * D_MODEL
EPS = 1e-6
SCALE = HEAD_DIM ** -0.5
NEG = -1e30
LOG2E = 1.4426950408889634

LANES = 128
GROUP_LANES = 256
N_GROUPS = D_MODEL // GROUP_LANES
POOL_HALO = 16

TM_MIX = 1024
SUB_MIX = 512
TM_MLP = 512
KV_CHUNK = 256
VMEM_LIMIT = 60 * 1024 * 1024


def _dot(a, b):
    return jnp.dot(a, b, preferred_element_type=F32)


def _rms(x, g):
    ms = jnp.mean(x * x, axis=-1, keepdims=True)
    return x * lax.rsqrt(ms + EPS) * g


def _half_rms_factor(sq, upper):
    lane = lax.broadcasted_iota(jnp.int32, (1, LANES), 1)
    keep = (lane >= HEAD_DIM) if upper else (lane < HEAD_DIM)
    ss = jnp.sum(jnp.where(keep, sq, 0.0), axis=-1, keepdims=True)
    return lax.rsqrt(ss * (1.0 / HEAD_DIM) + EPS)


def _half_rscale(sq, upper):
    lane = lax.broadcasted_iota(jnp.int32, (1, LANES), 1)
    keep = (lane >= HEAD_DIM) if upper else (lane < HEAD_DIM)
    ss = jnp.sum(jnp.where(keep, sq, 0.0), axis=-1, keepdims=True)
    return lax.rsqrt(ss + HEAD_DIM * EPS) * LOG2E


def _head_norm(x, gain):
    lower = lax.broadcasted_iota(jnp.int32, (1, LANES), 1) < HEAD_DIM
    cols = []
    for c in range(x.shape[1] // LANES):
        lanes = slice(c * LANES, (c + 1) * LANES)
        xc = x[:, lanes]
        sq = xc * xc
        r = jnp.where(lower, _half_rms_factor(sq, upper=False), _half_rms_factor(sq, upper=True))
        cols.append(xc * r * gain[:, lanes])
    return jnp.concatenate(cols, axis=1)


WOUT_ROWS_A = (
    [(g * POOL_GROUP_DIM, POOL_GROUP_DIM, g * GROUP_LANES) for g in range(POOL_GROUPS)]
    + [(MAIN_WIDTH + g * HEAD_DIM, HEAD_DIM, (g + 1) * GROUP_LANES - HEAD_DIM) for g in range(MEM_HEADS)])
WOUT_ROWS_B = (
    [((kvh * SWA_GROUP + g) * HEAD_DIM, HEAD_DIM, g * GROUP_LANES + kvh * HEAD_DIM)
     for kvh in range(SWA_KV_HEADS) for g in range(SWA_GROUP)]
    + [(MAIN_WIDTH, MEM_WIDTH, MAIN_WIDTH)])


def _permute_rows_bf16(dst_ref, src_ref, chunks):
    for src0, n, dst0 in chunks:
        dst_ref[dst0:dst0 + n] = src_ref[src0:src0 + n].astype(BF16)


def _softmax_rows_log2(s2):
    m = jnp.max(s2, axis=-1, keepdims=True)
    e = jnp.exp2(s2 - m)
    return e * (1.0 / jnp.sum(e, axis=-1, keepdims=True))


def _rows_only(a, r0, r1, dst0=None, n=None):
    n = a.shape[0] if n is None else n
    dst0 = r0 if dst0 is None else dst0
    pieces = []
    if dst0 > 0:
        pieces.append(jnp.zeros((dst0, a.shape[1]), a.dtype))
    pieces.append(a[r0:r1])
    rest = n - dst0 - (r1 - r0)
    if rest > 0:
        pieces.append(jnp.zeros((rest, a.shape[1]), a.dtype))
    return jnp.concatenate(pieces, axis=0)


def _mem_kv_kernel(mem_ref, g_ref, w_ref, gk_ref, mkt_ref, mvw_ref):
    m = mem_ref[0]
    lane_slot = lax.broadcasted_iota(jnp.int32, (1, MEM_WIDTH), 1) // HEAD_DIM
    last = MEM_HEADS - 1
    for l in range(DEPTH):
        xn = _rms(m, g_ref[l]).astype(BF16)
        kv = _dot(xn, w_ref[l].astype(BF16))
        mkt_ref[l, 0] = _head_norm(kv[:, :MEM_WIDTH], gk_ref[l]).T.astype(BF16)
        v = kv[:, MEM_WIDTH:]
        for hm in range(MEM_HEADS):
            if l < N_A:
                moved = v if hm == last else pltpu.roll(v, (last - hm) * HEAD_DIM, 1)
                blk = jnp.where(lane_slot == last, moved, 0.0)
            else:
                blk = jnp.where(lane_slot == hm, v, 0.0)
            mvw_ref[l, 0, :, hm * GROUP_LANES:(hm + 1) * GROUP_LANES] = blk.astype(BF16)


def _mem_kv(mem, mem_norm, w_mem_kv, gk_mem):
    B = mem.shape[0]
    const3 = lambda b: (0, 0, 0)
    return pl.pallas_call(
        _mem_kv_kernel,
        grid=(B,),
        in_specs=[
            pl.BlockSpec((1, MEM_LEN, D_MODEL), lambda b: (b, 0, 0)),
            pl.BlockSpec((DEPTH, 1, D_MODEL), const3),
            pl.BlockSpec((DEPTH, D_MODEL, 2 * MEM_WIDTH), const3, pipeline_mode=pl.Buffered(1)),
            pl.BlockSpec((DEPTH, 1, MEM_WIDTH), const3),
        ],
        out_specs=[
            pl.BlockSpec((DEPTH, 1, MEM_WIDTH, MEM_LEN), lambda b: (0, b, 0, 0)),
            pl.BlockSpec((DEPTH, 1, MEM_LEN, D_MODEL), lambda b: (0, b, 0, 0)),
        ],
        out_shape=[
            jax.ShapeDtypeStruct((DEPTH, B, MEM_WIDTH, MEM_LEN), BF16),
            jax.ShapeDtypeStruct((DEPTH, B, MEM_LEN, D_MODEL), BF16),
        ],
        compiler_params=pltpu.CompilerParams(
            dimension_semantics=("arbitrary",), vmem_limit_bytes=VMEM_LIMIT),
        name="mem_kv",
    )(mem, mem_norm, w_mem_kv, gk_mem)


def _mix_a_kernel(h_ref, g_ref, win_ref, gq_ref, poolw_ref, pscale_ref, mkt_ref, mvw_ref, wout_ref,
                  o_ref, halo_ref, ext_ref, cat_ref, woutp_ref, xn_ref):
    i = pl.program_id(1)
    tm = h_ref.shape[1]

    @pl.when(jnp.logical_and(pl.program_id(0) == 0, i == 0))
    def _():
        _permute_rows_bf16(woutp_ref, wout_ref, WOUT_ROWS_A)

    @pl.when(i == 0)
    def _():
        halo_ref[...] = jnp.zeros_like(halo_ref)

    mkt = mkt_ref[0, 0]
    t1 = (lax.broadcasted_iota(jnp.int32, (POOL_HALO, GROUP_LANES), 0) + (i * tm + 1)).astype(F32)
    nsub = tm // SUB_MIX

    def win_piece(sb, c):
        srows = slice(sb * SUB_MIX, (sb + 1) * SUB_MIX)
        lanes = slice(c * GROUP_LANES, (c + 1) * GROUP_LANES)
        if c == 0:
            xn_ref[srows] = _rms(h_ref[0, srows], g_ref[...]).astype(BF16)
        proj = _dot(xn_ref[srows], win_ref[:, lanes])
        if sb == 0:
            ext_ref[0:POOL_HALO, lanes] = halo_ref[:, lanes]
        ext_ref[POOL_HALO + sb * SUB_MIX:POOL_HALO + (sb + 1) * SUB_MIX, lanes] = proj
        if sb == nsub - 1:
            halo_ref[:, lanes] = proj[SUB_MIX - POOL_HALO:]

    def wout_piece(sb, c):
        srows = slice(sb * SUB_MIX, (sb + 1) * SUB_MIX)
        lanes = slice(c * GROUP_LANES, (c + 1) * GROUP_LANES)
        o_ref[0, srows, lanes] = h_ref[0, srows, lanes] + _dot(cat_ref[srows], woutp_ref[:, lanes])

    def group_block(sb, g):
        srows = slice(sb * SUB_MIX, (sb + 1) * SUB_MIX)
        erows = slice(POOL_HALO + sb * SUB_MIX, POOL_HALO + (sb + 1) * SUB_MIX)
        lanes = slice(g * GROUP_LANES, (g + 1) * GROUP_LANES)
        s = ext_ref[sb * SUB_MIX:POOL_HALO + (sb + 1) * SUB_MIX, lanes]
        for k in range(g + 1):
            s = s + pltpu.roll(s, 1 << k, 0)
        s = s[POOL_HALO:]
        win = float(POOL_WINDOWS[g])
        if sb == 0:
            pooled = jnp.concatenate(
                [s[:POOL_HALO] / jnp.minimum(t1, win), s[POOL_HALO:] * (1.0 / win)], axis=0)
        else:
            pooled = s * (1.0 / win)
        d = (pooled - ext_ref[erows, lanes]).astype(BF16)
        mixed = _dot(d, poolw_ref[g]) * pscale_ref[:, lanes]

        mlanes = slice((g + 1) * GROUP_LANES - LANES, (g + 1) * GROUP_LANES)
        qc = ext_ref[erows, mlanes]
        r = _half_rscale(qc * qc, upper=True)
        qn = (qc * gq_ref[:, mlanes]).astype(BF16)
        ktm = _rows_only(mkt, g * HEAD_DIM, (g + 1) * HEAD_DIM, dst0=LANES - HEAD_DIM, n=LANES)
        p = _softmax_rows_log2(_dot(qn, ktm) * r).astype(BF16)
        memo = _dot(p, mvw_ref[0, 0, :, lanes])
        cat_ref[srows, lanes] = (mixed + memo).astype(BF16)

    for c in range(N_GROUPS):
        win_piece(0, c)
    for sb in range(nsub):
        fill = []
        if sb + 1 < nsub:
            fill += [(win_piece, sb + 1, c) for c in range(N_GROUPS)]
        if sb >= 1:
            fill += [(wout_piece, sb - 1, c) for c in range(N_GROUPS)]
        per = -(-len(fill) // N_GROUPS)
        for g in range(N_GROUPS):
            for f in fill[g * per:(g + 1) * per]:
                f[0](*f[1:])
            group_block(sb, g)
    for c in range(N_GROUPS):
        wout_piece(nsub - 1, c)


def _mix_a(layer, h, g, w_in, gq, poolw, pscale, mkt, mvw, w_out):
    B, S, _ = h.shape
    tm = TM_MIX
    const2 = lambda b, i: (0, 0)
    return pl.pallas_call(
        _mix_a_kernel,
        grid=(B, S // tm),
        in_specs=[
            pl.BlockSpec((1, tm, D_MODEL), lambda b, i: (b, i, 0)),
            pl.BlockSpec((1, D_MODEL), const2),
            pl.BlockSpec((D_MODEL, D_MODEL), const2),
            pl.BlockSpec((1, D_MODEL), const2),
            pl.BlockSpec((N_GROUPS, GROUP_LANES, GROUP_LANES), lambda b, i: (0, 0, 0)),
            pl.BlockSpec((1, D_MODEL), const2),
            pl.BlockSpec((1, 1, MEM_WIDTH, MEM_LEN), lambda b, i: (layer, b, 0, 0)),
            pl.BlockSpec((1, 1, MEM_LEN, D_MODEL), lambda b, i: (layer, b, 0, 0)),
            pl.BlockSpec((None, D_MODEL, D_MODEL), lambda b, i: (layer, 0, 0), pipeline_mode=pl.Buffered(1)),
        ],
        out_specs=pl.BlockSpec((1, tm, D_MODEL), lambda b, i: (b, i, 0)),
        out_shape=jax.ShapeDtypeStruct(h.shape, F32),
        scratch_shapes=[
            pltpu.VMEM((POOL_HALO, D_MODEL), F32),
            pltpu.VMEM((tm + POOL_HALO, D_MODEL), F32),
            pltpu.VMEM((tm, D_MODEL), BF16),
            pltpu.VMEM((D_MODEL, D_MODEL), BF16),
            pltpu.VMEM((tm, D_MODEL), BF16),
        ],
        compiler_params=pltpu.CompilerParams(
            dimension_semantics=("arbitrary", "arbitrary"), vmem_limit_bytes=VMEM_LIMIT),
        name=f"mix_pool_{layer}",
    )(h, g, w_in, gq, poolw, pscale, mkt, mvw, w_out)


def _mix_b_kernel(sinks_ref, h_ref, g_ref, win_ref, gq_ref, ktc_ref, ktp_ref, vc_ref, vp_ref,
                  bias_ref, mkt_ref, mvw_ref, wout_ref, o_ref, xn_ref, proj_ref, cat_ref, woutp_ref):
    i = pl.program_id(1)
    tm = h_ref.shape[1]
    nq = tm // BLOCK

    @pl.when(jnp.logical_and(pl.program_id(0) == 0, i == 0))
    def _():
        _permute_rows_bf16(woutp_ref, wout_ref, WOUT_ROWS_B)

    tri = (lax.broadcasted_iota(jnp.int32, (BLOCK, BLOCK), 0)
           >= lax.broadcasted_iota(jnp.int32, (BLOCK, BLOCK), 1))
    pen = jnp.where(tri, 0.0, jnp.where(i == 0, NEG, 0.0))
    lower_b = jnp.where(tri, 1.0, 0.0).astype(BF16)
    upper_b = jnp.where(tri, 0.0, 1.0).astype(BF16)

    lane_head = lax.broadcasted_iota(jnp.int32, (BLOCK, KV_HALF), 1) // HEAD_DIM
    vblocks = [vp_ref[...]] + [vc_ref[b * BLOCK:(b + 1) * BLOCK] for b in range(nq)]
    vmask = [[jnp.where(lane_head == kvh, vb, jnp.zeros_like(vb)) for kvh in range(SWA_KV_HEADS)]
             for vb in vblocks]

    mkt = mkt_ref[0, 0]
    mvst = jnp.concatenate(
        [mvw_ref[0, 0, :, hm * GROUP_LANES:(hm + 1) * GROUP_LANES] for hm in range(MEM_HEADS)], axis=0)
    mem_lanes = slice(SWA_GROUP * GROUP_LANES, D_MODEL)

    def win_piece(sb, c):
        srows = slice(sb * SUB_MIX, (sb + 1) * SUB_MIX)
        lanes = slice(c * GROUP_LANES, (c + 1) * GROUP_LANES)
        if c == 0:
            xn_ref[srows] = _rms(h_ref[0, srows], g_ref[...]).astype(BF16)
        proj_ref[srows, lanes] = _dot(xn_ref[srows], win_ref[:, lanes])

    def wout_piece(sb, c):
        srows = slice(sb * SUB_MIX, (sb + 1) * SUB_MIX)
        lanes = slice(c * GROUP_LANES, (c + 1) * GROUP_LANES)
        o_ref[0, srows, lanes] = h_ref[0, srows, lanes] + _dot(cat_ref[srows], woutp_ref[:, lanes])

    def swa_block(qb):
        r0 = qb * BLOCK
        rows = slice(r0, r0 + BLOCK)
        if qb == 0:
            ktw = jnp.concatenate([ktp_ref[...], ktc_ref[:, 0:BLOCK]], axis=1)
        else:
            ktw = ktc_ref[:, r0 - BLOCK:r0 + BLOCK]
        pblk = proj_ref[rows, 0:MAIN_WIDTH]
        rcol = []
        for c in range(MAIN_WIDTH // LANES):
            pc = pblk[:, c * LANES:(c + 1) * LANES]
            sq = pc * pc
            rcol.append(_half_rscale(sq, upper=False))
            rcol.append(_half_rscale(sq, upper=True))
        qn = (pblk * gq_ref[:, 0:MAIN_WIDTH]).astype(BF16)
        qst = jnp.concatenate(
            [qn[:, g * GROUP_LANES:(g + 1) * GROUP_LANES] for g in range(SWA_GROUP)], axis=0)
        p_all = []
        v_all = []
        for kvh in range(SWA_KV_HEADS):
            ktm = _rows_only(ktw, kvh * HEAD_DIM, (kvh + 1) * HEAD_DIM)
            s_all = _dot(qst, ktm)
            ps = []
            for g in range(SWA_GROUP):
                hq = kvh * SWA_GROUP + g
                sg = s_all[g * BLOCK:(g + 1) * BLOCK]
                s = (jnp.where(tri, sg[:, BLOCK:], sg[:, :BLOCK]) * rcol[g * SWA_KV_HEADS + kvh]
                     + bias_ref[hq])
                if qb == 0:
                    s = s + pen
                sink = sinks_ref[hq] * LOG2E
                m = jnp.max(s, axis=-1, keepdims=True)
                e = jnp.exp2(s - m)
                den = jnp.sum(e, axis=-1, keepdims=True) + jnp.exp2(sink - m)
                p = (e * (1.0 / den)).astype(BF16)
                ps.append(jnp.concatenate([p * upper_b, p * lower_b], axis=1))
            p_all.append(jnp.concatenate(ps, axis=0))
            v_all.append(jnp.concatenate([vmask[qb][kvh], vmask[qb + 1][kvh]], axis=0))
        o = _dot(jnp.concatenate(p_all, axis=1), jnp.concatenate(v_all, axis=0))
        for g in range(SWA_GROUP):
            cat_ref[rows, g * GROUP_LANES:(g + 1) * GROUP_LANES] = o[g * BLOCK:(g + 1) * BLOCK].astype(BF16)

    def mem_block(sb):
        srows = slice(sb * SUB_MIX, (sb + 1) * SUB_MIX)
        pmem = proj_ref[srows, mem_lanes]
        qm = (pmem * gq_ref[:, mem_lanes]).astype(BF16)
        pm = []
        for hm in range(MEM_HEADS):
            pc = pmem[:, (hm // 2) * LANES:(hm // 2 + 1) * LANES]
            r = _half_rscale(pc * pc, upper=bool(hm % 2))
            ktm = _rows_only(mkt, hm * HEAD_DIM, (hm + 1) * HEAD_DIM)
            pm.append(_softmax_rows_log2(_dot(qm, ktm) * r).astype(BF16))
        cat_ref[srows, mem_lanes] = _dot(jnp.concatenate(pm, axis=1), mvst).astype(BF16)

    nsub = tm // SUB_MIX
    qps = SUB_MIX // BLOCK
    for c in range(N_GROUPS):
        win_piece(0, c)
    for sb in range(nsub):
        main = [(swa_block, qb) for qb in range(sb * qps, (sb + 1) * qps)] + [(mem_block, sb)]
        fill = []
        if sb + 1 < nsub:
            fill += [(win_piece, sb + 1, c) for c in range(N_GROUPS)]
        if sb >= 1:
            fill += [(wout_piece, sb - 1, c) for c in range(N_GROUPS)]
        per = -(-len(fill) // len(main))
        for k, item in enumerate(main):
            for f in fill[k * per:(k + 1) * per]:
                f[0](*f[1:])
            item[0](*item[1:])
    for c in range(N_GROUPS):
        wout_piece(nsub - 1, c)


def _mix_b(layer, h, sinks, g, w_in, gq, kt, v, bias, mkt, mvw, w_out):
    B, S, _ = h.shape
    tm = TM_MIX
    nq = tm // BLOCK
    nt = S // tm
    const2 = lambda b, i: (0, 0)
    return pl.pallas_call(
        _mix_b_kernel,
        grid=(B, S // tm),
        in_specs=[
            pl.BlockSpec(memory_space=pltpu.SMEM),
            pl.BlockSpec((1, tm, D_MODEL), lambda b, i: (b, i, 0)),
            pl.BlockSpec((1, D_MODEL), const2),
            pl.BlockSpec((D_MODEL, D_MODEL), const2),
            pl.BlockSpec((1, D_MODEL), const2),
            pl.BlockSpec((KV_HALF, tm), lambda b, i: (0, b * nt + i)),
            pl.BlockSpec((KV_HALF, BLOCK), lambda b, i: (0, b * nt * nq + jnp.maximum(i * nq - 1, 0))),
            pl.BlockSpec((tm, KV_HALF), lambda b, i: (b * nt + i, 0)),
            pl.BlockSpec((BLOCK, KV_HALF), lambda b, i: (b * nt * nq + jnp.maximum(i * nq - 1, 0), 0)),
            pl.BlockSpec((SWA_Q_HEADS, BLOCK, BLOCK), lambda b, i: (0, 0, 0)),
            pl.BlockSpec((1, 1, MEM_WIDTH, MEM_LEN), lambda b, i: (layer, b, 0, 0)),
            pl.BlockSpec((1, 1, MEM_LEN, D_MODEL), lambda b, i: (layer, b, 0, 0)),
            pl.BlockSpec((None, D_MODEL, D_MODEL), lambda b, i: (layer, 0, 0), pipeline_mode=pl.Buffered(1)),
        ],
        out_specs=pl.BlockSpec((1, tm, D_MODEL), lambda b, i: (b, i, 0)),
        out_shape=jax.ShapeDtypeStruct(h.shape, F32),
        scratch_shapes=[pltpu.VMEM((tm, D_MODEL), BF16), pltpu.VMEM((tm, D_MODEL), F32),
                        pltpu.VMEM((tm, D_MODEL), BF16), pltpu.VMEM((D_MODEL, D_MODEL), BF16)],
        compiler_params=pltpu.CompilerParams(
            dimension_semantics=("arbitrary", "arbitrary"), vmem_limit_bytes=VMEM_LIMIT),
        name=f"mix_swa_{layer}",
    )(sinks, h, g, w_in, gq, kt, kt, v, v, bias, mkt, mvw, w_out)


def _mlp_tile(h_ref, g_ref, wup_ref, wdn_ref):
    x = h_ref[...]
    xn = _rms(x, g_ref[...]).astype(BF16)
    u = jnp.maximum(_dot(xn, wup_ref[...].astype(BF16)), 0.0)
    return x + _dot((u * u).astype(BF16), wdn_ref[...].astype(BF16))


def _mlp_kernel(h_ref, g_ref, wup_ref, wdn_ref, o_ref):
    o_ref[...] = _mlp_tile(h_ref, g_ref, wup_ref, wdn_ref)


def _mlp_kv_kernel(h_ref, g_ref, wup_ref, wdn_ref, gkv_ref, wkv_ref, gk_ref, o_ref, kt_ref, v_ref):
    out = _mlp_tile(h_ref, g_ref, wup_ref, wdn_ref)
    o_ref[...] = out
    for r0 in range(0, out.shape[0], KV_CHUNK):
        rows = slice(r0, r0 + KV_CHUNK)
        kv = _dot(_rms(out[rows], gkv_ref[...]).astype(BF16), wkv_ref[...].astype(BF16))
        kn = _head_norm(kv[:, :KV_HALF], gk_ref[...])
        kt_ref[:, rows] = kn.T.astype(BF16)
        v_ref[rows] = kv[:, KV_HALF:].astype(BF16)


def _mlp(layer, h2d, g, w_up, w_down, kv_params=None):
    T = h2d.shape[0]
    tm = TM_MLP
    const = lambda i: (0, 0)
    in_specs = [
        pl.BlockSpec((tm, D_MODEL), lambda i: (i, 0)),
        pl.BlockSpec((1, D_MODEL), const),
        pl.BlockSpec((None, D_MODEL, D_FF), lambda i: (layer, 0, 0), pipeline_mode=pl.Buffered(1)),
        pl.BlockSpec((None, D_FF, D_MODEL), lambda i: (layer, 0, 0), pipeline_mode=pl.Buffered(1)),
    ]
    out_specs = pl.BlockSpec((tm, D_MODEL), lambda i: (i, 0))
    out_shape = jax.ShapeDtypeStruct(h2d.shape, F32)
    args = (h2d, g, w_up, w_down)
    body = _mlp_kernel
    if kv_params is not None:
        in_specs += [
            pl.BlockSpec((1, D_MODEL), const),
            pl.BlockSpec((D_MODEL, 2 * KV_HALF), const, pipeline_mode=pl.Buffered(1)),
            pl.BlockSpec((1, KV_HALF), const),
        ]
        out_specs = [out_specs, pl.BlockSpec((KV_HALF, tm), lambda i: (0, i)),
                     pl.BlockSpec((tm, KV_HALF), lambda i: (i, 0))]
        out_shape = [out_shape, jax.ShapeDtypeStruct((KV_HALF, T), BF16),
                     jax.ShapeDtypeStruct((T, KV_HALF), BF16)]
        args += tuple(kv_params)
        body = _mlp_kv_kernel
    return pl.pallas_call(
        body,
        grid=(T // tm,),
        in_specs=in_specs,
        out_specs=out_specs,
        out_shape=out_shape,
        compiler_params=pltpu.CompilerParams(
            dimension_semantics=("arbitrary",), vmem_limit_bytes=VMEM_LIMIT),
        name=f"mlp_{layer}",
    )(*args)


def _cols_a(main, memq):
    lead = main.shape[:-1]
    m = main.reshape(*lead, POOL_GROUPS, POOL_GROUP_DIM)
    q = memq.reshape(*lead, MEM_HEADS, HEAD_DIM)
    return jnp.concatenate([m, q], axis=-1).reshape(*lead, D_MODEL)


def _cols_b(main, memq):
    lead = main.shape[:-1]
    m = main.reshape(*lead, SWA_KV_HEADS, SWA_GROUP, HEAD_DIM)
    m = jnp.swapaxes(m, -3, -2).reshape(*lead, MAIN_WIDTH)
    return jnp.concatenate([m, memq], axis=-1)


def _perm_in(w_in_l, cols):
    return cols(w_in_l[:, :MAIN_WIDTH], w_in_l[:, MAIN_WIDTH:]).astype(BF16)


def _swa_bias():
    qi = jnp.arange(BLOCK, dtype=jnp.int32)[:, None]
    kj = jnp.arange(BLOCK, dtype=jnp.int32)[None, :]
    dist = jnp.where(kj <= qi, qi - kj, qi - kj + BLOCK)
    slopes = jnp.exp2(-8.0 * jnp.arange(1, SWA_Q_HEADS + 1, dtype=F32) / SWA_Q_HEADS)
    return -(slopes[:, None, None] * dist.astype(F32)[None]) * LOG2E


def kernel(x, mem, norm_mix, w_in, pool_w, pool_scale, kv_norm, w_kv, k_norm, q_norm, sinks, mem_norm,
           w_mem_kv, mem_q_norm, mem_k_norm, w_out, norm_mlp, w_up, w_down):
    B, S, _ = x.shape
    zeros_main = jnp.zeros((MAIN_WIDTH,), F32)
    zeros_mem = jnp.zeros((MEM_WIDTH,), F32)

    gk_mem = jnp.tile(mem_k_norm, (1, MEM_HEADS)).reshape(DEPTH, 1, MEM_WIDTH)
    mkt, mvw = _mem_kv(mem, mem_norm.reshape(DEPTH, 1, D_MODEL), w_mem_kv, gk_mem)

    bias = _swa_bias()

    h = x
    kt = v = None
    for l in range(DEPTH):
        g_mix = norm_mix[l].reshape(1, D_MODEL)
        if l < N_A:
            w_in_p = _perm_in(w_in[l], _cols_a)
            gq = _cols_a(zeros_main, jnp.tile(mem_q_norm[l], MEM_HEADS)).reshape(1, D_MODEL)
            poolw = jnp.pad(pool_w[l], ((0, 0), (0, GROUP_LANES - POOL_GROUP_DIM),
                                        (0, GROUP_LANES - POOL_GROUP_DIM))).astype(BF16)
            pscale = _cols_a(pool_scale[l], zeros_mem).reshape(1, D_MODEL)
            h = _mix_a(l, h, g_mix, w_in_p, gq, poolw, pscale, mkt, mvw, w_out)
        else:
            j = l - N_A
            w_in_p = _perm_in(w_in[l], _cols_b)
            gq = _cols_b(jnp.tile(q_norm[j], SWA_Q_HEADS), jnp.tile(mem_q_norm[l], MEM_HEADS))
            h = _mix_b(l, h, sinks[j], g_mix, w_in_p, gq.reshape(1, D_MODEL), kt, v, bias,
                       mkt, mvw, w_out)
        mlp_args = (l, h.reshape(B * S, D_MODEL), norm_mlp[l].reshape(1, D_MODEL),
                    w_up, w_down)
        if l == N_A - 1:
            gk = jnp.tile(k_norm, SWA_KV_HEADS).reshape(1, KV_HALF)
            h2d, kt, v = _mlp(*mlp_args, kv_params=(kv_norm.reshape(1, D_MODEL), w_kv, gk))
        else:
            h2d = _mlp(*mlp_args)
        h = h2d.reshape(B, S, D_MODEL)
    return h
```

```python
import jax
import jax.numpy as jnp
from jax import lax
from jax.experimental import pallas as pl
from jax.experimental.pallas import tpu as pltpu

F32 = jnp.float32
BF16 = jnp.bfloat16

D_MODEL = 1024
DEPTH = 4
N_A = DEPTH // 2
HEAD_DIM = 64
MEM_LEN = 256
MEM_HEADS = 4
MEM_WIDTH = MEM_HEADS * HEAD_DIM
MAIN_WIDTH = D_MODEL - MEM_WIDTH
POOL_WINDOWS = (2, 4, 8, 16)
POOL_GROUPS = len(POOL_WINDOWS)
POOL_GROUP_DIM = MAIN_WIDTH // POOL_GROUPS
SWA_Q_HEADS = MAIN_WIDTH // HEAD_DIM
SWA_KV_HEADS = 4
SWA_GROUP = SWA_Q_HEADS // SWA_KV_HEADS
KV_HALF = SWA_KV_HEADS * HEAD_DIM
BLOCK = 128
D_FF = 4 * D_MODEL
EPS = 1e-6
SCALE = HEAD_DIM ** -0.5
NEG = -1e30
LOG2E = 1.4426950408889634

LANES = 128
GROUP_LANES = 256
N_GROUPS = D_MODEL // GROUP_LANES
POOL_HALO = 16

TM_MIX = 1024
SUB_MIX = 512
TM_MLP = 1024
TM_MLP_KV = 512
KV_CHUNK = 256
VMEM_LIMIT = 60 * 1024 * 1024


def _dot(a, b):
    return jnp.dot(a, b, preferred_element_type=F32)


def _rms(x, g):
    ms = jnp.mean(x * x, axis=-1, keepdims=True)
    return x * lax.rsqrt(ms + EPS) * g


def _half_rms_factor(sq, upper):
    lane = lax.broadcasted_iota(jnp.int32, (1, LANES), 1)
    keep = (lane >= HEAD_DIM) if upper else (lane < HEAD_DIM)
    ss = jnp.sum(jnp.where(keep, sq, 0.0), axis=-1, keepdims=True)
    return lax.rsqrt(ss * (1.0 / HEAD_DIM) + EPS)


def _half_rscale(sq, upper):
    lane = lax.broadcasted_iota(jnp.int32, (1, LANES), 1)
    keep = (lane >= HEAD_DIM) if upper else (lane < HEAD_DIM)
    ss = jnp.sum(jnp.where(keep, sq, 0.0), axis=-1, keepdims=True)
    return lax.rsqrt(ss + HEAD_DIM * EPS) * LOG2E


def _head_norm(x, gain):
    lower = lax.broadcasted_iota(jnp.int32, (1, LANES), 1) < HEAD_DIM
    cols = []
    for c in range(x.shape[1] // LANES):
        lanes = slice(c * LANES, (c + 1) * LANES)
        xc = x[:, lanes]
        sq = xc * xc
        r = jnp.where(lower, _half_rms_factor(sq, upper=False), _half_rms_factor(sq, upper=True))
        cols.append(xc * r * gain[:, lanes])
    return jnp.concatenate(cols, axis=1)


WOUT_ROWS_A = (
    [(g * POOL_GROUP_DIM, POOL_GROUP_DIM, g * GROUP_LANES) for g in range(POOL_GROUPS)]
    + [(MAIN_WIDTH + g * HEAD_DIM, HEAD_DIM, (g + 1) * GROUP_LANES - HEAD_DIM) for g in range(MEM_HEADS)])
WOUT_ROWS_B = (
    [((kvh * SWA_GROUP + g) * HEAD_DIM, HEAD_DIM, g * GROUP_LANES + kvh * HEAD_DIM)
     for kvh in range(SWA_KV_HEADS) for g in range(SWA_GROUP)]
    + [(MAIN_WIDTH, MEM_WIDTH, MAIN_WIDTH)])


def _permute_rows_bf16(dst_ref, src_ref, chunks):
    for src0, n, dst0 in chunks:
        dst_ref[dst0:dst0 + n] = src_ref[src0:src0 + n].astype(BF16)


def _softmax_rows_log2(s2):
    m = jnp.max(s2, axis=-1, keepdims=True)
    e = jnp.exp2(s2 - m)
    return e * (1.0 / jnp.sum(e, axis=-1, keepdims=True))


def _rows_only(a, r0, r1, dst0=None, n=None):
    n = a.shape[0] if n is None else n
    dst0 = r0 if dst0 is None else dst0
    pieces = []
    if dst0 > 0:
        pieces.append(jnp.zeros((dst0, a.shape[1]), a.dtype))
    pieces.append(a[r0:r1])
    rest = n - dst0 - (r1 - r0)
    if rest > 0:
        pieces.append(jnp.zeros((rest, a.shape[1]), a.dtype))
    return jnp.concatenate(pieces, axis=0)


def _mem_kv_kernel(mem_ref, g_ref, w_ref, gk_ref, mkt_ref, mvw_ref):
    m = mem_ref[0]
    lane_slot = lax.broadcasted_iota(jnp.int32, (1, MEM_WIDTH), 1) // HEAD_DIM
    last = MEM_HEADS - 1
    for l in range(DEPTH):
        xn = _rms(m, g_ref[l]).astype(BF16)
        kv = _dot(xn, w_ref[l].astype(BF16))
        mkt_ref[l, 0] = _head_norm(kv[:, :MEM_WIDTH], gk_ref[l]).T.astype(BF16)
        v = kv[:, MEM_WIDTH:]
        for hm in range(MEM_HEADS):
            if l < N_A:
                moved = v if hm == last else pltpu.roll(v, (last - hm) * HEAD_DIM, 1)
                blk = jnp.where(lane_slot == last, moved, 0.0)
            else:
                blk = jnp.where(lane_slot == hm, v, 0.0)
            mvw_ref[l, 0, :, hm * GROUP_LANES:(hm + 1) * GROUP_LANES] = blk.astype(BF16)


def _mem_kv(mem, mem_norm, w_mem_kv, gk_mem):
    B = mem.shape[0]
    const3 = lambda b: (0, 0, 0)
    return pl.pallas_call(
        _mem_kv_kernel,
        grid=(B,),
        in_specs=[
            pl.BlockSpec((1, MEM_LEN, D_MODEL), lambda b: (b, 0, 0)),
            pl.BlockSpec((DEPTH, 1, D_MODEL), const3),
            pl.BlockSpec((DEPTH, D_MODEL, 2 * MEM_WIDTH), const3, pipeline_mode=pl.Buffered(1)),
            pl.BlockSpec((DEPTH, 1, MEM_WIDTH), const3),
        ],
        out_specs=[
            pl.BlockSpec((DEPTH, 1, MEM_WIDTH, MEM_LEN), lambda b: (0, b, 0, 0)),
            pl.BlockSpec((DEPTH, 1, MEM_LEN, D_MODEL), lambda b: (0, b, 0, 0)),
        ],
        out_shape=[
            jax.ShapeDtypeStruct((DEPTH, B, MEM_WIDTH, MEM_LEN), BF16),
            jax.ShapeDtypeStruct((DEPTH, B, MEM_LEN, D_MODEL), BF16),
        ],
        compiler_params=pltpu.CompilerParams(
            dimension_semantics=("arbitrary",), vmem_limit_bytes=VMEM_LIMIT),
        name="mem_kv",
    )(mem, mem_norm, w_mem_kv, gk_mem)


def _mix_a_kernel(h_ref, g_ref, win_ref, gq_ref, poolw_ref, pscale_ref, mkt_ref, mvw_ref, wout_ref,
                  o_ref, halo_ref, ext_ref, cat_ref, woutp_ref, xn_ref):
    i = pl.program_id(1)
    tm = h_ref.shape[1]

    @pl.when(jnp.logical_and(pl.program_id(0) == 0, i == 0))
    def _():
        _permute_rows_bf16(woutp_ref, wout_ref, WOUT_ROWS_A)

    @pl.when(i == 0)
    def _():
        halo_ref[...] = jnp.zeros_like(halo_ref)

    mkt = mkt_ref[0, 0]
    t1 = (lax.broadcasted_iota(jnp.int32, (POOL_HALO, GROUP_LANES), 0) + (i * tm + 1)).astype(F32)
    nsub = tm // SUB_MIX

    def win_piece(sb, c):
        srows = slice(sb * SUB_MIX, (sb + 1) * SUB_MIX)
        lanes = slice(c * GROUP_LANES, (c + 1) * GROUP_LANES)
        if c == 0:
            xn_ref[srows] = _rms(h_ref[0, srows], g_ref[...]).astype(BF16)
        proj = _dot(xn_ref[srows], win_ref[:, lanes])
        if sb == 0:
            ext_ref[0:POOL_HALO, lanes] = halo_ref[:, lanes]
        ext_ref[POOL_HALO + sb * SUB_MIX:POOL_HALO + (sb + 1) * SUB_MIX, lanes] = proj
        if sb == nsub - 1:
            halo_ref[:, lanes] = proj[SUB_MIX - POOL_HALO:]

    def wout_piece(sb, c):
        srows = slice(sb * SUB_MIX, (sb + 1) * SUB_MIX)
        lanes = slice(c * GROUP_LANES, (c + 1) * GROUP_LANES)
        o_ref[0, srows, lanes] = h_ref[0, srows, lanes] + _dot(cat_ref[srows], woutp_ref[:, lanes])

    def group_block(sb, g):
        srows = slice(sb * SUB_MIX, (sb + 1) * SUB_MIX)
        erows = slice(POOL_HALO + sb * SUB_MIX, POOL_HALO + (sb + 1) * SUB_MIX)
        lanes = slice(g * GROUP_LANES, (g + 1) * GROUP_LANES)
        s = ext_ref[sb * SUB_MIX:POOL_HALO + (sb + 1) * SUB_MIX, lanes]
        for k in range(g + 1):
            s = s + pltpu.roll(s, 1 << k, 0)
        s = s[POOL_HALO:]
        win = float(POOL_WINDOWS[g])
        if sb == 0:
            pooled = jnp.concatenate(
                [s[:POOL_HALO] / jnp.minimum(t1, win), s[POOL_HALO:] * (1.0 / win)], axis=0)
        else:
            pooled = s * (1.0 / win)
        d = (pooled - ext_ref[erows, lanes]).astype(BF16)
        mixed = _dot(d, poolw_ref[g]) * pscale_ref[:, lanes]

        mlanes = slice((g + 1) * GROUP_LANES - LANES, (g + 1) * GROUP_LANES)
        qc = ext_ref[erows, mlanes]
        r = _half_rscale(qc * qc, upper=True)
        qn = (qc * gq_ref[:, mlanes]).astype(BF16)
        ktm = _rows_only(mkt, g * HEAD_DIM, (g + 1) * HEAD_DIM, dst0=LANES - HEAD_DIM, n=LANES)
        p = _softmax_rows_log2(_dot(qn, ktm) * r).astype(BF16)
        memo = _dot(p, mvw_ref[0, 0, :, lanes])
        cat_ref[srows, lanes] = (mixed + memo).astype(BF16)

    for c in range(N_GROUPS):
        win_piece(0, c)
    for sb in range(nsub):
        fill = []
        if sb + 1 < nsub:
            fill += [(win_piece, sb + 1, c) for c in range(N_GROUPS)]
        if sb >= 1:
            fill += [(wout_piece, sb - 1, c) for c in range(N_GROUPS)]
        per = -(-len(fill) // N_GROUPS)
        for g in range(N_GROUPS):
            for f in fill[g * per:(g + 1) * per]:
                f[0](*f[1:])
            group_block(sb, g)
    for c in range(N_GROUPS):
        wout_piece(nsub - 1, c)


def _mix_a(layer, h, g, w_in, gq, poolw, pscale, mkt, mvw, w_out):
    B, S, _ = h.shape
    tm = TM_MIX
    const2 = lambda b, i: (0, 0)
    return pl.pallas_call(
        _mix_a_kernel,
        grid=(B, S // tm),
        in_specs=[
            pl.BlockSpec((1, tm, D_MODEL), lambda b, i: (b, i, 0)),
            pl.BlockSpec((1, D_MODEL), const2),
            pl.BlockSpec((D_MODEL, D_MODEL), const2),
            pl.BlockSpec((1, D_MODEL), const2),
            pl.BlockSpec((N_GROUPS, GROUP_LANES, GROUP_LANES), lambda b, i: (0, 0, 0)),
            pl.BlockSpec((1, D_MODEL), const2),
            pl.BlockSpec((1, 1, MEM_WIDTH, MEM_LEN), lambda b, i: (layer, b, 0, 0)),
            pl.BlockSpec((1, 1, MEM_LEN, D_MODEL), lambda b, i: (layer, b, 0, 0)),
            pl.BlockSpec((None, D_MODEL, D_MODEL), lambda b, i: (layer, 0, 0), pipeline_mode=pl.Buffered(1)),
        ],
        out_specs=pl.BlockSpec((1, tm, D_MODEL), lambda b, i: (b, i, 0)),
        out_shape=jax.ShapeDtypeStruct(h.shape, F32),
        scratch_shapes=[
            pltpu.VMEM((POOL_HALO, D_MODEL), F32),
            pltpu.VMEM((tm + POOL_HALO, D_MODEL), F32),
            pltpu.VMEM((tm, D_MODEL), BF16),
            pltpu.VMEM((D_MODEL, D_MODEL), BF16),
            pltpu.VMEM((tm, D_MODEL), BF16),
        ],
        compiler_params=pltpu.CompilerParams(
            dimension_semantics=("arbitrary", "arbitrary"), vmem_limit_bytes=VMEM_LIMIT),
        name=f"mix_pool_{layer}",
    )(h, g, w_in, gq, poolw, pscale, mkt, mvw, w_out)


def _mix_b_kernel(sinks_ref, h_ref, g_ref, win_ref, gq_ref, ktc_ref, ktp_ref, vc_ref, vp_ref,
                  bias_ref, mkt_ref, mvw_ref, wout_ref, o_ref, xn_ref, proj_ref, cat_ref, woutp_ref):
    i = pl.program_id(1)
    tm = h_ref.shape[1]
    nq = tm // BLOCK

    @pl.when(jnp.logical_and(pl.program_id(0) == 0, i == 0))
    def _():
        _permute_rows_bf16(woutp_ref, wout_ref, WOUT_ROWS_B)

    tri = (lax.broadcasted_iota(jnp.int32, (BLOCK, BLOCK), 0)
           >= lax.broadcasted_iota(jnp.int32, (BLOCK, BLOCK), 1))
    pen = jnp.where(tri, 0.0, jnp.where(i == 0, NEG, 0.0))
    lower_b = jnp.where(tri, 1.0, 0.0).astype(BF16)
    upper_b = jnp.where(tri, 0.0, 1.0).astype(BF16)

    lane_head = lax.broadcasted_iota(jnp.int32, (BLOCK, KV_HALF), 1) // HEAD_DIM
    vblocks = [vp_ref[...]] + [vc_ref[b * BLOCK:(b + 1) * BLOCK] for b in range(nq)]
    vmask = [[jnp.where(lane_head == kvh, vb, jnp.zeros_like(vb)) for kvh in range(SWA_KV_HEADS)]
             for vb in vblocks]

    mkt = mkt_ref[0, 0]
    mvst = jnp.concatenate(
        [mvw_ref[0, 0, :, hm * GROUP_LANES:(hm + 1) * GROUP_LANES] for hm in range(MEM_HEADS)], axis=0)
    mem_lanes = slice(SWA_GROUP * GROUP_LANES, D_MODEL)

    def win_piece(sb, c):
        srows = slice(sb * SUB_MIX, (sb + 1) * SUB_MIX)
        lanes = slice(c * GROUP_LANES, (c + 1) * GROUP_LANES)
        if c == 0:
            xn_ref[srows] = _rms(h_ref[0, srows], g_ref[...]).astype(BF16)
        proj_ref[srows, lanes] = _dot(xn_ref[srows], win_ref[:, lanes])

    def wout_piece(sb, c):
        srows = slice(sb * SUB_MIX, (sb + 1) * SUB_MIX)
        lanes = slice(c * GROUP_LANES, (c + 1) * GROUP_LANES)
        o_ref[0, srows, lanes] = h_ref[0, srows, lanes] + _dot(cat_ref[srows], woutp_ref[:, lanes])

    def swa_block(qb):
        r0 = qb * BLOCK
        rows = slice(r0, r0 + BLOCK)
        if qb == 0:
            ktw = jnp.concatenate([ktp_ref[...], ktc_ref[:, 0:BLOCK]], axis=1)
        else:
            ktw = ktc_ref[:, r0 - BLOCK:r0 + BLOCK]
        pblk = proj_ref[rows, 0:MAIN_WIDTH]
        rcol = []
        for c in range(MAIN_WIDTH // LANES):
            pc = pblk[:, c * LANES:(c + 1) * LANES]
            sq = pc * pc
            rcol.append(_half_rscale(sq, upper=False))
            rcol.append(_half_rscale(sq, upper=True))
        qn = (pblk * gq_ref[:, 0:MAIN_WIDTH]).astype(BF16)
        qst = jnp.concatenate(
            [qn[:, g * GROUP_LANES:(g + 1) * GROUP_LANES] for g in range(SWA_GROUP)], axis=0)
        p_all = []
        v_all = []
        for kvh in range(SWA_KV_HEADS):
            ktm = _rows_only(ktw, kvh * HEAD_DIM, (kvh + 1) * HEAD_DIM)
            s_all = _dot(qst, ktm)
            ps = []
            for g in range(SWA_GROUP):
                hq = kvh * SWA_GROUP + g
                sg = s_all[g * BLOCK:(g + 1) * BLOCK]
                s = (jnp.where(tri, sg[:, BLOCK:], sg[:, :BLOCK]) * rcol[g * SWA_KV_HEADS + kvh]
                     + bias_ref[hq])
                if qb == 0:
                    s = s + pen
                sink = sinks_ref[hq] * LOG2E
                m = jnp.max(s, axis=-1, keepdims=True)
                e = jnp.exp2(s - m)
                den = jnp.sum(e, axis=-1, keepdims=True) + jnp.exp2(sink - m)
                p = (e * (1.0 / den)).astype(BF16)
                ps.append(jnp.concatenate([p * upper_b, p * lower_b], axis=1))
            p_all.append(jnp.concatenate(ps, axis=0))
            v_all.append(jnp.concatenate([vmask[qb][kvh], vmask[qb + 1][kvh]], axis=0))
        o = _dot(jnp.concatenate(p_all, axis=1), jnp.concatenate(v_all, axis=0))
        for g in range(SWA_GROUP):
            cat_ref[rows, g * GROUP_LANES:(g + 1) * GROUP_LANES] = o[g * BLOCK:(g + 1) * BLOCK].astype(BF16)

    def mem_block(sb):
        srows = slice(sb * SUB_MIX, (sb + 1) * SUB_MIX)
        pmem = proj_ref[srows, mem_lanes]
        qm = (pmem * gq_ref[:, mem_lanes]).astype(BF16)
        pm = []
        for hm in range(MEM_HEADS):
            pc = pmem[:, (hm // 2) * LANES:(hm // 2 + 1) * LANES]
            r = _half_rscale(pc * pc, upper=bool(hm % 2))
            ktm = _rows_only(mkt, hm * HEAD_DIM, (hm + 1) * HEAD_DIM)
            pm.append(_softmax_rows_log2(_dot(qm, ktm) * r).astype(BF16))
        cat_ref[srows, mem_lanes] = _dot(jnp.concatenate(pm, axis=1), mvst).astype(BF16)

    nsub = tm // SUB_MIX
    qps = SUB_MIX // BLOCK
    for c in range(N_GROUPS):
        win_piece(0, c)
    for sb in range(nsub):
        main = [(swa_block, qb) for qb in range(sb * qps, (sb + 1) * qps)] + [(mem_block, sb)]
        fill = []
        if sb + 1 < nsub:
            fill += [(win_piece, sb + 1, c) for c in range(N_GROUPS)]
        if sb >= 1:
            fill += [(wout_piece, sb - 1, c) for c in range(N_GROUPS)]
        per = -(-len(fill) // len(main))
        for k, item in enumerate(main):
            for f in fill[k * per:(k + 1) * per]:
                f[0](*f[1:])
            item[0](*item[1:])
    for c in range(N_GROUPS):
        wout_piece(nsub - 1, c)


def _mix_b(layer, h, sinks, g, w_in, gq, kt, v, bias, mkt, mvw, w_out):
    B, S, _ = h.shape
    tm = TM_MIX
    nq = tm // BLOCK
    nt = S // tm
    const2 = lambda b, i: (0, 0)
    return pl.pallas_call(
        _mix_b_kernel,
        grid=(B, S // tm),
        in_specs=[
            pl.BlockSpec(memory_space=pltpu.SMEM),
            pl.BlockSpec((1, tm, D_MODEL), lambda b, i: (b, i, 0)),
            pl.BlockSpec((1, D_MODEL), const2),
            pl.BlockSpec((D_MODEL, D_MODEL), const2),
            pl.BlockSpec((1, D_MODEL), const2),
            pl.BlockSpec((KV_HALF, tm), lambda b, i: (0, b * nt + i)),
            pl.BlockSpec((KV_HALF, BLOCK), lambda b, i: (0, b * nt * nq + jnp.maximum(i * nq - 1, 0))),
            pl.BlockSpec((tm, KV_HALF), lambda b, i: (b * nt + i, 0)),
            pl.BlockSpec((BLOCK, KV_HALF), lambda b, i: (b * nt * nq + jnp.maximum(i * nq - 1, 0), 0)),
            pl.BlockSpec((SWA_Q_HEADS, BLOCK, BLOCK), lambda b, i: (0, 0, 0)),
            pl.BlockSpec((1, 1, MEM_WIDTH, MEM_LEN), lambda b, i: (layer, b, 0, 0)),
            pl.BlockSpec((1, 1, MEM_LEN, D_MODEL), lambda b, i: (layer, b, 0, 0)),
            pl.BlockSpec((None, D_MODEL, D_MODEL), lambda b, i: (layer, 0, 0), pipeline_mode=pl.Buffered(1)),
        ],
        out_specs=pl.BlockSpec((1, tm, D_MODEL), lambda b, i: (b, i, 0)),
        out_shape=jax.ShapeDtypeStruct(h.shape, F32),
        scratch_shapes=[pltpu.VMEM((tm, D_MODEL), BF16), pltpu.VMEM((tm, D_MODEL), F32),
                        pltpu.VMEM((tm, D_MODEL), BF16), pltpu.VMEM((D_MODEL, D_MODEL), BF16)],
        compiler_params=pltpu.CompilerParams(
            dimension_semantics=("arbitrary", "arbitrary"), vmem_limit_bytes=VMEM_LIMIT),
        name=f"mix_swa_{layer}",
    )(sinks, h, g, w_in, gq, kt, kt, v, v, bias, mkt, mvw, w_out)


def _mlp_tile(h_ref, g_ref, wup_ref, wdn_ref):
    x = h_ref[...]
    xn = _rms(x, g_ref[...]).astype(BF16)
    u = jnp.maximum(_dot(xn, wup_ref[...].astype(BF16)), 0.0)
    return x + _dot((u * u).astype(BF16), wdn_ref[...].astype(BF16))


def _mlp_kernel(h_ref, g_ref, wup_ref, wdn_ref, o_ref):
    o_ref[...] = _mlp_tile(h_ref, g_ref, wup_ref, wdn_ref)


def _mlp_kv_kernel(h_ref, g_ref, wup_ref, wdn_ref, gkv_ref, wkv_ref, gk_ref, o_ref, kt_ref, v_ref):
    out = _mlp_tile(h_ref, g_ref, wup_ref, wdn_ref)
    o_ref[...] = out
    for r0 in range(0, out.shape[0], KV_CHUNK):
        rows = slice(r0, r0 + KV_CHUNK)
        kv = _dot(_rms(out[rows], gkv_ref[...]).astype(BF16), wkv_ref[...].astype(BF16))
        kn = _head_norm(kv[:, :KV_HALF], gk_ref[...])
        kt_ref[:, rows] = kn.T.astype(BF16)
        v_ref[rows] = kv[:, KV_HALF:].astype(BF16)


def _mlp(layer, h2d, g, w_up, w_down, kv_params=None):
    T = h2d.shape[0]
    tm = TM_MLP if kv_params is None else TM_MLP_KV
    const = lambda i: (0, 0)
    in_specs = [
        pl.BlockSpec((tm, D_MODEL), lambda i: (i, 0)),
        pl.BlockSpec((1, D_MODEL), const),
        pl.BlockSpec((None, D_MODEL, D_FF), lambda i: (layer, 0, 0), pipeline_mode=pl.Buffered(1)),
        pl.BlockSpec((None, D_FF, D_MODEL), lambda i: (layer, 0, 0), pipeline_mode=pl.Buffered(1)),
    ]
    out_specs = pl.BlockSpec((tm, D_MODEL), lambda i: (i, 0))
    out_shape = jax.ShapeDtypeStruct(h2d.shape, F32)
    args = (h2d, g, w_up, w_down)
    body = _mlp_kernel
    if kv_params is not None:
        in_specs += [
            pl.BlockSpec((1, D_MODEL), const),
            pl.BlockSpec((D_MODEL, 2 * KV_HALF), const, pipeline_mode=pl.Buffered(1)),
            pl.BlockSpec((1, KV_HALF), const),
        ]
        out_specs = [out_specs, pl.BlockSpec((KV_HALF, tm), lambda i: (0, i)),
                     pl.BlockSpec((tm, KV_HALF), lambda i: (i, 0))]
        out_shape = [out_shape, jax.ShapeDtypeStruct((KV_HALF, T), BF16),
                     jax.ShapeDtypeStruct((T, KV_HALF), BF16)]
        args += tuple(kv_params)
        body = _mlp_kv_kernel
    return pl.pallas_call(
        body,
        grid=(T // tm,),
        in_specs=in_specs,
        out_specs=out_specs,
        out_shape=out_shape,
        compiler_params=pltpu.CompilerParams(
            dimension_semantics=("arbitrary",), vmem_limit_bytes=VMEM_LIMIT),
        name=f"mlp_{layer}",
    )(*args)


def _cols_a(main, memq):
    lead = main.shape[:-1]
    m = main.reshape(*lead, POOL_GROUPS, POOL_GROUP_DIM)
    q = memq.reshape(*lead, MEM_HEADS, HEAD_DIM)
    return jnp.concatenate([m, q], axis=-1).reshape(*lead, D_MODEL)


def _cols_b(main, memq):
    lead = main.shape[:-1]
    m = main.reshape(*lead, SWA_KV_HEADS, SWA_GROUP, HEAD_DIM)
    m = jnp.swapaxes(m, -3, -2).reshape(*lead, MAIN_WIDTH)
    return jnp.concatenate([m, memq], axis=-1)


def _perm_in(w_in_l, cols):
    return cols(w_in_l[:, :MAIN_WIDTH], w_in_l[:, MAIN_WIDTH:]).astype(BF16)


def _swa_bias():
    qi = jnp.arange(BLOCK, dtype=jnp.int32)[:, None]
    kj = jnp.arange(BLOCK, dtype=jnp.int32)[None, :]
    dist = jnp.where(kj <= qi, qi - kj, qi - kj + BLOCK)
    slopes = jnp.exp2(-8.0 * jnp.arange(1, SWA_Q_HEADS + 1, dtype=F32) / SWA_Q_HEADS)
    return -(slopes[:, None, None] * dist.astype(F32)[None]) * LOG2E


def kernel(x, mem, norm_mix, w_in, pool_w, pool_scale, kv_norm, w_kv, k_norm, q_norm, sinks, mem_norm,
           w_mem_kv, mem_q_norm, mem_k_norm, w_out, norm_mlp, w_up, w_down):
    B, S, _ = x.shape
    zeros_main = jnp.zeros((MAIN_WIDTH,), F32)
    zeros_mem = jnp.zeros((MEM_WIDTH,), F32)

    gk_mem = jnp.tile(mem_k_norm, (1, MEM_HEADS)).reshape(DEPTH, 1, MEM_WIDTH)
    mkt, mvw = _mem_kv(mem, mem_norm.reshape(DEPTH, 1, D_MODEL), w_mem_kv, gk_mem)

    bias = _swa_bias()

    h = x
    kt = v = None
    for l in range(DEPTH):
        g_mix = norm_mix[l].reshape(1, D_MODEL)
        if l < N_A:
            w_in_p = _perm_in(w_in[l], _cols_a)
            gq = _cols_a(zeros_main, jnp.tile(mem_q_norm[l], MEM_HEADS)).reshape(1, D_MODEL)
            poolw = jnp.pad(pool_w[l], ((0, 0), (0, GROUP_LANES - POOL_GROUP_DIM),
                                        (0, GROUP_LANES - POOL_GROUP_DIM))).astype(BF16)
            pscale = _cols_a(pool_scale[l], zeros_mem).reshape(1, D_MODEL)
            h = _mix_a(l, h, g_mix, w_in_p, gq, poolw, pscale, mkt, mvw, w_out)
        else:
            j = l - N_A
            w_in_p = _perm_in(w_in[l], _cols_b)
            gq = _cols_b(jnp.tile(q_norm[j], SWA_Q_HEADS), jnp.tile(mem_q_norm[l], MEM_HEADS))
            h = _mix_b(l, h, sinks[j], g_mix, w_in_p, gq.reshape(1, D_MODEL), kt, v, bias,
                       mkt, mvw, w_out)
        mlp_args = (l, h.reshape(B * S, D_MODEL), norm_mlp[l].reshape(1, D_MODEL),
                    w_up, w_down)
        if l == N_A - 1:
            gk = jnp.tile(k_norm, SWA_KV_HEADS).reshape(1, KV_HALF)
            h2d, kt, v = _mlp(*mlp_args, kv_params=(kv_norm.reshape(1, D_MODEL), w_kv, gk))
        else:
            h2d = _mlp(*mlp_args)
        h = h2d.reshape(B, S, D_MODEL)
    return h
```

```python
import jax
import jax.numpy as jnp
from jax import lax
from jax.experimental import pallas as pl
from jax.experimental.pallas import tpu as pltpu

F32 = jnp.float32
BF16 = jnp.bfloat16

D_MODEL = 1024
DEPTH = 4
N_A = DEPTH // 2
HEAD_DIM = 64
MEM_LEN = 256
MEM_HEADS = 4
MEM_WIDTH = MEM_HEADS * HEAD_DIM
MAIN_WIDTH = D_MODEL - MEM_WIDTH
POOL_WINDOWS = (2, 4, 8, 16)
POOL_GROUPS = len(POOL_WINDOWS)
POOL_GROUP_DIM = MAIN_WIDTH // POOL_GROUPS
SWA_Q_HEADS = MAIN_WIDTH // HEAD_DIM
SWA_KV_HEADS = 4
SWA_GROUP = SWA_Q_HEADS // SWA_KV_HEADS
KV_HALF = SWA_KV_HEADS * HEAD_DIM
BLOCK = 128
D_FF = 4 * D_MODEL
EPS = 1e-6
SCALE = HEAD_DIM ** -0.5
NEG = -1e30
LOG2E = 1.4426950408889634

LANES = 128
GROUP_LANES = 256
N_GROUPS = D_MODEL // GROUP_LANES
POOL_HALO = 16

TM_MIX = 1024
SUB_MIX = 512
TM_MLP = 1024
KV_CHUNK = 256
VMEM_LIMIT = 60 * 1024 * 1024


def _dot(a, b):
    return jnp.dot(a, b, preferred_element_type=F32)


def _rms(x, g):
    ms = jnp.mean(x * x, axis=-1, keepdims=True)
    return x * lax.rsqrt(ms + EPS) * g


def _half_rms_factor(sq, upper):
    lane = lax.broadcasted_iota(jnp.int32, (1, LANES), 1)
    keep = (lane >= HEAD_DIM) if upper else (lane < HEAD_DIM)
    ss = jnp.sum(jnp.where(keep, sq, 0.0), axis=-1, keepdims=True)
    return lax.rsqrt(ss * (1.0 / HEAD_DIM) + EPS)


def _half_rscale(sq, upper):
    lane = lax.broadcasted_iota(jnp.int32, (1, LANES), 1)
    keep = (lane >= HEAD_DIM) if upper else (lane < HEAD_DIM)
    ss = jnp.sum(jnp.where(keep, sq, 0.0), axis=-1, keepdims=True)
    return lax.rsqrt(ss + HEAD_DIM * EPS) * LOG2E


def _head_norm(x, gain):
    lower = lax.broadcasted_iota(jnp.int32, (1, LANES), 1) < HEAD_DIM
    cols = []
    for c in range(x.shape[1] // LANES):
        lanes = slice(c * LANES, (c + 1) * LANES)
        xc = x[:, lanes]
        sq = xc * xc
        r = jnp.where(lower, _half_rms_factor(sq, upper=False), _half_rms_factor(sq, upper=True))
        cols.append(xc * r * gain[:, lanes])
    return jnp.concatenate(cols, axis=1)


WOUT_ROWS_A = (
    [(g * POOL_GROUP_DIM, POOL_GROUP_DIM, g * GROUP_LANES) for g in range(POOL_GROUPS)]
    + [(MAIN_WIDTH + g * HEAD_DIM, HEAD_DIM, (g + 1) * GROUP_LANES - HEAD_DIM) for g in range(MEM_HEADS)])
WOUT_ROWS_B = (
    [((kvh * SWA_GROUP + g) * HEAD_DIM, HEAD_DIM, g * GROUP_LANES + kvh * HEAD_DIM)
     for kvh in range(SWA_KV_HEADS) for g in range(SWA_GROUP)]
    + [(MAIN_WIDTH, MEM_WIDTH, MAIN_WIDTH)])


def _permute_rows_bf16(dst_ref, src_ref, chunks):
    for src0, n, dst0 in chunks:
        dst_ref[dst0:dst0 + n] = src_ref[src0:src0 + n].astype(BF16)


def _softmax_rows_log2(s2):
    m = jnp.max(s2, axis=-1, keepdims=True)
    e = jnp.exp2(s2 - m)
    return e * (1.0 / jnp.sum(e, axis=-1, keepdims=True))


def _rows_only(a, r0, r1, dst0=None, n=None):
    n = a.shape[0] if n is None else n
    dst0 = r0 if dst0 is None else dst0
    pieces = []
    if dst0 > 0:
        pieces.append(jnp.zeros((dst0, a.shape[1]), a.dtype))
    pieces.append(a[r0:r1])
    rest = n - dst0 - (r1 - r0)
    if rest > 0:
        pieces.append(jnp.zeros((rest, a.shape[1]), a.dtype))
    return jnp.concatenate(pieces, axis=0)


def _mem_kv_kernel(mem_ref, g_ref, w_ref, gk_ref, mkt_ref, mvw_ref):
    m = mem_ref[0]
    lane_slot = lax.broadcasted_iota(jnp.int32, (1, MEM_WIDTH), 1) // HEAD_DIM
    last = MEM_HEADS - 1
    for l in range(DEPTH):
        xn = _rms(m, g_ref[l]).astype(BF16)
        kv = _dot(xn, w_ref[l].astype(BF16))
        mkt_ref[l, 0] = _head_norm(kv[:, :MEM_WIDTH], gk_ref[l]).T.astype(BF16)
        v = kv[:, MEM_WIDTH:]
        for hm in range(MEM_HEADS):
            if l < N_A:
                moved = v if hm == last else pltpu.roll(v, (last - hm) * HEAD_DIM, 1)
                blk = jnp.where(lane_slot == last, moved, 0.0)
            else:
                blk = jnp.where(lane_slot == hm, v, 0.0)
            mvw_ref[l, 0, :, hm * GROUP_LANES:(hm + 1) * GROUP_LANES] = blk.astype(BF16)


def _mem_kv(mem, mem_norm, w_mem_kv, gk_mem):
    B = mem.shape[0]
    const3 = lambda b: (0, 0, 0)
    return pl.pallas_call(
        _mem_kv_kernel,
        grid=(B,),
        in_specs=[
            pl.BlockSpec((1, MEM_LEN, D_MODEL), lambda b: (b, 0, 0)),
            pl.BlockSpec((DEPTH, 1, D_MODEL), const3),
            pl.BlockSpec((DEPTH, D_MODEL, 2 * MEM_WIDTH), const3, pipeline_mode=pl.Buffered(1)),
            pl.BlockSpec((DEPTH, 1, MEM_WIDTH), const3),
        ],
        out_specs=[
            pl.BlockSpec((DEPTH, 1, MEM_WIDTH, MEM_LEN), lambda b: (0, b, 0, 0)),
            pl.BlockSpec((DEPTH, 1, MEM_LEN, D_MODEL), lambda b: (0, b, 0, 0)),
        ],
        out_shape=[
            jax.ShapeDtypeStruct((DEPTH, B, MEM_WIDTH, MEM_LEN), BF16),
            jax.ShapeDtypeStruct((DEPTH, B, MEM_LEN, D_MODEL), BF16),
        ],
        compiler_params=pltpu.CompilerParams(
            dimension_semantics=("arbitrary",), vmem_limit_bytes=VMEM_LIMIT),
        name="mem_kv",
    )(mem, mem_norm, w_mem_kv, gk_mem)


def _mix_a_kernel(h_ref, g_ref, win_ref, gq_ref, poolw_ref, pscale_ref, mkt_ref, mvw_ref, wout_ref,
                  wupf_ref, wdnf_ref, o_ref, wupb_ref, wdnb_ref, halo_ref, ext_ref, cat_ref, woutp_ref, xn_ref):
    i = pl.program_id(1)
    tm = h_ref.shape[1]
    wupb_ref[...] = wupf_ref[...].astype(BF16)
    wdnb_ref[...] = wdnf_ref[...].astype(BF16)

    @pl.when(jnp.logical_and(pl.program_id(0) == 0, i == 0))
    def _():
        _permute_rows_bf16(woutp_ref, wout_ref, WOUT_ROWS_A)

    @pl.when(i == 0)
    def _():
        halo_ref[...] = jnp.zeros_like(halo_ref)

    mkt = mkt_ref[0, 0]
    t1 = (lax.broadcasted_iota(jnp.int32, (POOL_HALO, GROUP_LANES), 0) + (i * tm + 1)).astype(F32)
    nsub = tm // SUB_MIX

    def win_piece(sb, c):
        srows = slice(sb * SUB_MIX, (sb + 1) * SUB_MIX)
        lanes = slice(c * GROUP_LANES, (c + 1) * GROUP_LANES)
        if c == 0:
            xn_ref[srows] = _rms(h_ref[0, srows], g_ref[...]).astype(BF16)
        proj = _dot(xn_ref[srows], win_ref[:, lanes])
        if sb == 0:
            ext_ref[0:POOL_HALO, lanes] = halo_ref[:, lanes]
        ext_ref[POOL_HALO + sb * SUB_MIX:POOL_HALO + (sb + 1) * SUB_MIX, lanes] = proj
        if sb == nsub - 1:
            halo_ref[:, lanes] = proj[SUB_MIX - POOL_HALO:]

    def wout_piece(sb, c):
        srows = slice(sb * SUB_MIX, (sb + 1) * SUB_MIX)
        lanes = slice(c * GROUP_LANES, (c + 1) * GROUP_LANES)
        o_ref[0, srows, lanes] = h_ref[0, srows, lanes] + _dot(cat_ref[srows], woutp_ref[:, lanes])

    def group_block(sb, g):
        srows = slice(sb * SUB_MIX, (sb + 1) * SUB_MIX)
        erows = slice(POOL_HALO + sb * SUB_MIX, POOL_HALO + (sb + 1) * SUB_MIX)
        lanes = slice(g * GROUP_LANES, (g + 1) * GROUP_LANES)
        s = ext_ref[sb * SUB_MIX:POOL_HALO + (sb + 1) * SUB_MIX, lanes]
        for k in range(g + 1):
            s = s + pltpu.roll(s, 1 << k, 0)
        s = s[POOL_HALO:]
        win = float(POOL_WINDOWS[g])
        if sb == 0:
            pooled = jnp.concatenate(
                [s[:POOL_HALO] / jnp.minimum(t1, win), s[POOL_HALO:] * (1.0 / win)], axis=0)
        else:
            pooled = s * (1.0 / win)
        d = (pooled - ext_ref[erows, lanes]).astype(BF16)
        mixed = _dot(d, poolw_ref[g]) * pscale_ref[:, lanes]

        mlanes = slice((g + 1) * GROUP_LANES - LANES, (g + 1) * GROUP_LANES)
        qc = ext_ref[erows, mlanes]
        r = _half_rscale(qc * qc, upper=True)
        qn = (qc * gq_ref[:, mlanes]).astype(BF16)
        ktm = _rows_only(mkt, g * HEAD_DIM, (g + 1) * HEAD_DIM, dst0=LANES - HEAD_DIM, n=LANES)
        p = _softmax_rows_log2(_dot(qn, ktm) * r).astype(BF16)
        memo = _dot(p, mvw_ref[0, 0, :, lanes])
        cat_ref[srows, lanes] = (mixed + memo).astype(BF16)

    for c in range(N_GROUPS):
        win_piece(0, c)
    for sb in range(nsub):
        fill = []
        if sb + 1 < nsub:
            fill += [(win_piece, sb + 1, c) for c in range(N_GROUPS)]
        if sb >= 1:
            fill += [(wout_piece, sb - 1, c) for c in range(N_GROUPS)]
        per = -(-len(fill) // N_GROUPS)
        for g in range(N_GROUPS):
            for f in fill[g * per:(g + 1) * per]:
                f[0](*f[1:])
            group_block(sb, g)
    for c in range(N_GROUPS):
        wout_piece(nsub - 1, c)


def _mlp_cast_specs(layer, n_steps, step_of):
    up_rows, dn_rows = D_MODEL // n_steps, D_FF // n_steps
    in_specs = [
        pl.BlockSpec((None, up_rows, D_FF), lambda b, i: (layer, step_of(b, i), 0)),
        pl.BlockSpec((None, dn_rows, D_MODEL), lambda b, i: (layer, step_of(b, i), 0)),
    ]
    out_specs = [
        pl.BlockSpec((up_rows, D_FF), lambda b, i: (step_of(b, i), 0)),
        pl.BlockSpec((dn_rows, D_MODEL), lambda b, i: (step_of(b, i), 0)),
    ]
    out_shape = [jax.ShapeDtypeStruct((D_MODEL, D_FF), BF16), jax.ShapeDtypeStruct((D_FF, D_MODEL), BF16)]
    return in_specs, out_specs, out_shape


def _mix_a(layer, h, g, w_in, gq, poolw, pscale, mkt, mvw, w_out, w_up, w_down):
    B, S, _ = h.shape
    tm = TM_MIX
    nt = S // tm
    const2 = lambda b, i: (0, 0)
    cast_in, cast_out, cast_shape = _mlp_cast_specs(layer, B * nt, lambda b, i: b * nt + i)
    return pl.pallas_call(
        _mix_a_kernel,
        grid=(B, S // tm),
        in_specs=[
            pl.BlockSpec((1, tm, D_MODEL), lambda b, i: (b, i, 0)),
            pl.BlockSpec((1, D_MODEL), const2),
            pl.BlockSpec((D_MODEL, D_MODEL), const2),
            pl.BlockSpec((1, D_MODEL), const2),
            pl.BlockSpec((N_GROUPS, GROUP_LANES, GROUP_LANES), lambda b, i: (0, 0, 0)),
            pl.BlockSpec((1, D_MODEL), const2),
            pl.BlockSpec((1, 1, MEM_WIDTH, MEM_LEN), lambda b, i: (layer, b, 0, 0)),
            pl.BlockSpec((1, 1, MEM_LEN, D_MODEL), lambda b, i: (layer, b, 0, 0)),
            pl.BlockSpec((None, D_MODEL, D_MODEL), lambda b, i: (layer, 0, 0), pipeline_mode=pl.Buffered(1)),
        ] + cast_in,
        out_specs=[pl.BlockSpec((1, tm, D_MODEL), lambda b, i: (b, i, 0))] + cast_out,
        out_shape=[jax.ShapeDtypeStruct(h.shape, F32)] + cast_shape,
        scratch_shapes=[
            pltpu.VMEM((POOL_HALO, D_MODEL), F32),
            pltpu.VMEM((tm + POOL_HALO, D_MODEL), F32),
            pltpu.VMEM((tm, D_MODEL), BF16),
            pltpu.VMEM((D_MODEL, D_MODEL), BF16),
            pltpu.VMEM((tm, D_MODEL), BF16),
        ],
        compiler_params=pltpu.CompilerParams(
            dimension_semantics=("arbitrary", "arbitrary"), vmem_limit_bytes=VMEM_LIMIT),
        name=f"mix_pool_{layer}",
    )(h, g, w_in, gq, poolw, pscale, mkt, mvw, w_out, w_up, w_down)


def _mix_b_kernel(sinks_ref, h_ref, g_ref, win_ref, gq_ref, ktc_ref, ktp_ref, vc_ref, vp_ref,
                  bias_ref, mkt_ref, mvw_ref, wout_ref, wupf_ref, wdnf_ref, o_ref, wupb_ref, wdnb_ref,
                  xn_ref, proj_ref, cat_ref, woutp_ref):
    i = pl.program_id(1)
    tm = h_ref.shape[1]
    nq = tm // BLOCK
    wupb_ref[...] = wupf_ref[...].astype(BF16)
    wdnb_ref[...] = wdnf_ref[...].astype(BF16)

    @pl.when(jnp.logical_and(pl.program_id(0) == 0, i == 0))
    def _():
        _permute_rows_bf16(woutp_ref, wout_ref, WOUT_ROWS_B)

    tri = (lax.broadcasted_iota(jnp.int32, (BLOCK, BLOCK), 0)
           >= lax.broadcasted_iota(jnp.int32, (BLOCK, BLOCK), 1))
    pen = jnp.where(tri, 0.0, jnp.where(i == 0, NEG, 0.0))
    lower_b = jnp.where(tri, 1.0, 0.0).astype(BF16)
    upper_b = jnp.where(tri, 0.0, 1.0).astype(BF16)

    lane_head = lax.broadcasted_iota(jnp.int32, (BLOCK, KV_HALF), 1) // HEAD_DIM
    vblocks = [vp_ref[...]] + [vc_ref[b * BLOCK:(b + 1) * BLOCK] for b in range(nq)]
    vmask = [[jnp.where(lane_head == kvh, vb, jnp.zeros_like(vb)) for kvh in range(SWA_KV_HEADS)]
             for vb in vblocks]

    mkt = mkt_ref[0, 0]
    mvst = jnp.concatenate(
        [mvw_ref[0, 0, :, hm * GROUP_LANES:(hm + 1) * GROUP_LANES] for hm in range(MEM_HEADS)], axis=0)
    mem_lanes = slice(SWA_GROUP * GROUP_LANES, D_MODEL)

    def win_piece(sb, c):
        srows = slice(sb * SUB_MIX, (sb + 1) * SUB_MIX)
        lanes = slice(c * GROUP_LANES, (c + 1) * GROUP_LANES)
        if c == 0:
            xn_ref[srows] = _rms(h_ref[0, srows], g_ref[...]).astype(BF16)
        proj_ref[srows, lanes] = _dot(xn_ref[srows], win_ref[:, lanes])

    def wout_piece(sb, c):
        srows = slice(sb * SUB_MIX, (sb + 1) * SUB_MIX)
        lanes = slice(c * GROUP_LANES, (c + 1) * GROUP_LANES)
        o_ref[0, srows, lanes] = h_ref[0, srows, lanes] + _dot(cat_ref[srows], woutp_ref[:, lanes])

    def swa_block(qb):
        r0 = qb * BLOCK
        rows = slice(r0, r0 + BLOCK)
        if qb == 0:
            ktw = jnp.concatenate([ktp_ref[...], ktc_ref[:, 0:BLOCK]], axis=1)
        else:
            ktw = ktc_ref[:, r0 - BLOCK:r0 + BLOCK]
        pblk = proj_ref[rows, 0:MAIN_WIDTH]
        rcol = []
        for c in range(MAIN_WIDTH // LANES):
            pc = pblk[:, c * LANES:(c + 1) * LANES]
            sq = pc * pc
            rcol.append(_half_rscale(sq, upper=False))
            rcol.append(_half_rscale(sq, upper=True))
        qn = (pblk * gq_ref[:, 0:MAIN_WIDTH]).astype(BF16)
        qst = jnp.concatenate(
            [qn[:, g * GROUP_LANES:(g + 1) * GROUP_LANES] for g in range(SWA_GROUP)], axis=0)
        p_all = []
        v_all = []
        for kvh in range(SWA_KV_HEADS):
            ktm = _rows_only(ktw, kvh * HEAD_DIM, (kvh + 1) * HEAD_DIM)
            s_all = _dot(qst, ktm)
            ps = []
            for g in range(SWA_GROUP):
                hq = kvh * SWA_GROUP + g
                sg = s_all[g * BLOCK:(g + 1) * BLOCK]
                s = (jnp.where(tri, sg[:, BLOCK:], sg[:, :BLOCK]) * rcol[g * SWA_KV_HEADS + kvh]
                     + bias_ref[hq])
                if qb == 0:
                    s = s + pen
                sink = sinks_ref[hq] * LOG2E
                m = jnp.max(s, axis=-1, keepdims=True)
                e = jnp.exp2(s - m)
                den = jnp.sum(e, axis=-1, keepdims=True) + jnp.exp2(sink - m)
                p = (e * (1.0 / den)).astype(BF16)
                ps.append(jnp.concatenate([p * upper_b, p * lower_b], axis=1))
            p_all.append(jnp.concatenate(ps, axis=0))
            v_all.append(jnp.concatenate([vmask[qb][kvh], vmask[qb + 1][kvh]], axis=0))
        o = _dot(jnp.concatenate(p_all, axis=1), jnp.concatenate(v_all, axis=0))
        for g in range(SWA_GROUP):
            cat_ref[rows, g * GROUP_LANES:(g + 1) * GROUP_LANES] = o[g * BLOCK:(g + 1) * BLOCK].astype(BF16)

    def mem_block(sb):
        srows = slice(sb * SUB_MIX, (sb + 1) * SUB_MIX)
        pmem = proj_ref[srows, mem_lanes]
        qm = (pmem * gq_ref[:, mem_lanes]).astype(BF16)
        pm = []
        for hm in range(MEM_HEADS):
            pc = pmem[:, (hm // 2) * LANES:(hm // 2 + 1) * LANES]
            r = _half_rscale(pc * pc, upper=bool(hm % 2))
            ktm = _rows_only(mkt, hm * HEAD_DIM, (hm + 1) * HEAD_DIM)
            pm.append(_softmax_rows_log2(_dot(qm, ktm) * r).astype(BF16))
        cat_ref[srows, mem_lanes] = _dot(jnp.concatenate(pm, axis=1), mvst).astype(BF16)

    nsub = tm // SUB_MIX
    qps = SUB_MIX // BLOCK
    for c in range(N_GROUPS):
        win_piece(0, c)
    for sb in range(nsub):
        main = [(swa_block, qb) for qb in range(sb * qps, (sb + 1) * qps)] + [(mem_block, sb)]
        fill = []
        if sb + 1 < nsub:
            fill += [(win_piece, sb + 1, c) for c in range(N_GROUPS)]
        if sb >= 1:
            fill += [(wout_piece, sb - 1, c) for c in range(N_GROUPS)]
        per = -(-len(fill) // len(main))
        for k, item in enumerate(main):
            for f in fill[k * per:(k + 1) * per]:
                f[0](*f[1:])
            item[0](*item[1:])
    for c in range(N_GROUPS):
        wout_piece(nsub - 1, c)


def _mix_b(layer, h, sinks, g, w_in, gq, kt, v, bias, mkt, mvw, w_out, w_up, w_down):
    B, S, _ = h.shape
    tm = TM_MIX
    nq = tm // BLOCK
    nt = S // tm
    const2 = lambda b, i: (0, 0)
    cast_in, cast_out, cast_shape = _mlp_cast_specs(layer, B * nt, lambda b, i: b * nt + i)
    return pl.pallas_call(
        _mix_b_kernel,
        grid=(B, S // tm),
        in_specs=[
            pl.BlockSpec(memory_space=pltpu.SMEM),
            pl.BlockSpec((1, tm, D_MODEL), lambda b, i: (b, i, 0)),
            pl.BlockSpec((1, D_MODEL), const2),
            pl.BlockSpec((D_MODEL, D_MODEL), const2),
            pl.BlockSpec((1, D_MODEL), const2),
            pl.BlockSpec((KV_HALF, tm), lambda b, i: (0, b * nt + i)),
            pl.BlockSpec((KV_HALF, BLOCK), lambda b, i: (0, b * nt * nq + jnp.maximum(i * nq - 1, 0))),
            pl.BlockSpec((tm, KV_HALF), lambda b, i: (b * nt + i, 0)),
            pl.BlockSpec((BLOCK, KV_HALF), lambda b, i: (b * nt * nq + jnp.maximum(i * nq - 1, 0), 0)),
            pl.BlockSpec((SWA_Q_HEADS, BLOCK, BLOCK), lambda b, i: (0, 0, 0)),
            pl.BlockSpec((1, 1, MEM_WIDTH, MEM_LEN), lambda b, i: (layer, b, 0, 0)),
            pl.BlockSpec((1, 1, MEM_LEN, D_MODEL), lambda b, i: (layer, b, 0, 0)),
            pl.BlockSpec((None, D_MODEL, D_MODEL), lambda b, i: (layer, 0, 0), pipeline_mode=pl.Buffered(1)),
        ] + cast_in,
        out_specs=[pl.BlockSpec((1, tm, D_MODEL), lambda b, i: (b, i, 0))] + cast_out,
        out_shape=[jax.ShapeDtypeStruct(h.shape, F32)] + cast_shape,
        scratch_shapes=[pltpu.VMEM((tm, D_MODEL), BF16), pltpu.VMEM((tm, D_MODEL), F32),
                        pltpu.VMEM((tm, D_MODEL), BF16), pltpu.VMEM((D_MODEL, D_MODEL), BF16)],
        compiler_params=pltpu.CompilerParams(
            dimension_semantics=("arbitrary", "arbitrary"), vmem_limit_bytes=VMEM_LIMIT),
        name=f"mix_swa_{layer}",
    )(sinks, h, g, w_in, gq, kt, kt, v, v, bias, mkt, mvw, w_out, w_up, w_down)


def _mlp_tile(h_ref, g_ref, wup_ref, wdn_ref):
    x = h_ref[...]
    xn = _rms(x, g_ref[...]).astype(BF16)
    u = jnp.maximum(_dot(xn, wup_ref[...]), 0.0)
    return x + _dot((u * u).astype(BF16), wdn_ref[...])


def _mlp_kernel(h_ref, g_ref, wup_ref, wdn_ref, o_ref):
    o_ref[...] = _mlp_tile(h_ref, g_ref, wup_ref, wdn_ref)


def _mlp_kv_kernel(h_ref, g_ref, wup_ref, wdn_ref, gkv_ref, wkv_ref, gk_ref, o_ref, kt_ref, v_ref):
    out = _mlp_tile(h_ref, g_ref, wup_ref, wdn_ref)
    o_ref[...] = out
    for r0 in range(0, out.shape[0], KV_CHUNK):
        rows = slice(r0, r0 + KV_CHUNK)
        kv = _dot(_rms(out[rows], gkv_ref[...]).astype(BF16), wkv_ref[...].astype(BF16))
        kn = _head_norm(kv[:, :KV_HALF], gk_ref[...])
        kt_ref[:, rows] = kn.T.astype(BF16)
        v_ref[rows] = kv[:, KV_HALF:].astype(BF16)


def _mlp(layer, h2d, g, w_up, w_down, kv_params=None):
    T = h2d.shape[0]
    tm = TM_MLP
    const = lambda i: (0, 0)
    in_specs = [
        pl.BlockSpec((tm, D_MODEL), lambda i: (i, 0)),
        pl.BlockSpec((1, D_MODEL), const),
        pl.BlockSpec((D_MODEL, D_FF), const, pipeline_mode=pl.Buffered(1)),
        pl.BlockSpec((D_FF, D_MODEL), const, pipeline_mode=pl.Buffered(1)),
    ]
    out_specs = pl.BlockSpec((tm, D_MODEL), lambda i: (i, 0))
    out_shape = jax.ShapeDtypeStruct(h2d.shape, F32)
    args = (h2d, g, w_up, w_down)
    body = _mlp_kernel
    if kv_params is not None:
        in_specs += [
            pl.BlockSpec((1, D_MODEL), const),
            pl.BlockSpec((D_MODEL, 2 * KV_HALF), const, pipeline_mode=pl.Buffered(1)),
            pl.BlockSpec((1, KV_HALF), const),
        ]
        out_specs = [out_specs, pl.BlockSpec((KV_HALF, tm), lambda i: (0, i)),
                     pl.BlockSpec((tm, KV_HALF), lambda i: (i, 0))]
        out_shape = [out_shape, jax.ShapeDtypeStruct((KV_HALF, T), BF16),
                     jax.ShapeDtypeStruct((T, KV_HALF), BF16)]
        args += tuple(kv_params)
        body = _mlp_kv_kernel
    return pl.pallas_call(
        body,
        grid=(T // tm,),
        in_specs=in_specs,
        out_specs=out_specs,
        out_shape=out_shape,
        compiler_params=pltpu.CompilerParams(
            dimension_semantics=("arbitrary",), vmem_limit_bytes=VMEM_LIMIT),
        name=f"mlp_{layer}",
    )(*args)


def _cols_a(main, memq):
    lead = main.shape[:-1]
    m = main.reshape(*lead, POOL_GROUPS, POOL_GROUP_DIM)
    q = memq.reshape(*lead, MEM_HEADS, HEAD_DIM)
    return jnp.concatenate([m, q], axis=-1).reshape(*lead, D_MODEL)


def _cols_b(main, memq):
    lead = main.shape[:-1]
    m = main.reshape(*lead, SWA_KV_HEADS, SWA_GROUP, HEAD_DIM)
    m = jnp.swapaxes(m, -3, -2).reshape(*lead, MAIN_WIDTH)
    return jnp.concatenate([m, memq], axis=-1)


def _perm_in(w_in_l, cols):
    return cols(w_in_l[:, :MAIN_WIDTH], w_in_l[:, MAIN_WIDTH:]).astype(BF16)


def _swa_bias():
    qi = jnp.arange(BLOCK, dtype=jnp.int32)[:, None]
    kj = jnp.arange(BLOCK, dtype=jnp.int32)[None, :]
    dist = jnp.where(kj <= qi, qi - kj, qi - kj + BLOCK)
    slopes = jnp.exp2(-8.0 * jnp.arange(1, SWA_Q_HEADS + 1, dtype=F32) / SWA_Q_HEADS)
    return -(slopes[:, None, None] * dist.astype(F32)[None]) * LOG2E


def kernel(x, mem, norm_mix, w_in, pool_w, pool_scale, kv_norm, w_kv, k_norm, q_norm, sinks, mem_norm,
           w_mem_kv, mem_q_norm, mem_k_norm, w_out, norm_mlp, w_up, w_down):
    B, S, _ = x.shape
    zeros_main = jnp.zeros((MAIN_WIDTH,), F32)
    zeros_mem = jnp.zeros((MEM_WIDTH,), F32)

    gk_mem = jnp.tile(mem_k_norm, (1, MEM_HEADS)).reshape(DEPTH, 1, MEM_WIDTH)
    mkt, mvw = _mem_kv(mem, mem_norm.reshape(DEPTH, 1, D_MODEL), w_mem_kv, gk_mem)

    bias = _swa_bias()

    h = x
    kt = v = None
    for l in range(DEPTH):
        g_mix = norm_mix[l].reshape(1, D_MODEL)
        if l < N_A:
            w_in_p = _perm_in(w_in[l], _cols_a)
            gq = _cols_a(zeros_main, jnp.tile(mem_q_norm[l], MEM_HEADS)).reshape(1, D_MODEL)
            poolw = jnp.pad(pool_w[l], ((0, 0), (0, GROUP_LANES - POOL_GROUP_DIM),
                                        (0, GROUP_LANES - POOL_GROUP_DIM))).astype(BF16)
            pscale = _cols_a(pool_scale[l], zeros_mem).reshape(1, D_MODEL)
            h, w_up_b, w_down_b = _mix_a(l, h, g_mix, w_in_p, gq, poolw, pscale, mkt, mvw, w_out, w_up, w_down)
        else:
            j = l - N_A
            w_in_p = _perm_in(w_in[l], _cols_b)
            gq = _cols_b(jnp.tile(q_norm[j], SWA_Q_HEADS), jnp.tile(mem_q_norm[l], MEM_HEADS))
            h, w_up_b, w_down_b = _mix_b(l, h, sinks[j], g_mix, w_in_p, gq.reshape(1, D_MODEL), kt, v, bias,
                                         mkt, mvw, w_out, w_up, w_down)
        mlp_args = (l, h.reshape(B * S, D_MODEL), norm_mlp[l].reshape(1, D_MODEL), w_up_b, w_down_b)
        if l == N_A - 1:
            gk = jnp.tile(k_norm, SWA_KV_HEADS).reshape(1, KV_HALF)
            h2d, kt, v = _mlp(*mlp_args, kv_params=(kv_norm.reshape(1, D_MODEL), w_kv, gk))
        else:
            h2d = _mlp(*mlp_args)
        h = h2d.reshape(B, S, D_MODEL)
    return h
```

```python
import jax
import jax.numpy as jnp
from jax import lax
from jax.experimental import pallas as pl
from jax.experimental.pallas import tpu as pltpu

F32 = jnp.float32
BF16 = jnp.bfloat16

D_MODEL = 1024
DEPTH = 4
N_A = DEPTH // 2
HEAD_DIM = 64
MEM_LEN = 256
MEM_HEADS = 4
MEM_WIDTH = MEM_HEADS * HEAD_DIM
MAIN_WIDTH = D_MODEL - MEM_WIDTH
POOL_WINDOWS = (2, 4, 8, 16)
POOL_GROUPS = len(POOL_WINDOWS)
POOL_GROUP_DIM = MAIN_WIDTH // POOL_GROUPS
SWA_Q_HEADS = MAIN_WIDTH // HEAD_DIM
SWA_KV_HEADS = 4
SWA_GROUP = SWA_Q_HEADS // SWA_KV_HEADS
KV_HALF = SWA_KV_HEADS * HEAD_DIM
BLOCK = 128
D_FF = 4 * D_MODEL
EPS = 1e-6
SCALE = HEAD_DIM ** -0.5
NEG = -1e30
LOG2E = 1.4426950408889634

LANES = 128
GROUP_LANES = 256
N_GROUPS = D_MODEL // GROUP_LANES
POOL_HALO = 16

TM_MIX = 1024
SUB_MIX = 512
TM_MLP = 1024
KV_CHUNK = 256
VMEM_LIMIT = 52 * 1024 * 1024

assert abs(SCALE * SCALE * HEAD_DIM - 1.0) < 1e-12


def _dot(a, b):
    return jnp.dot(a, b, preferred_element_type=F32)


def _rms(x, g):
    ms = jnp.mean(x * x, axis=-1, keepdims=True)
    return x * lax.rsqrt(ms + EPS) * g


def _half_sum(sq, upper):
    lane = lax.broadcasted_iota(jnp.int32, (1, LANES), 1)
    keep = (lane >= HEAD_DIM) if upper else (lane < HEAD_DIM)
    return jnp.sum(jnp.where(keep, sq, 0.0), axis=-1, keepdims=True)


def _half_rms_factor(sq, upper):
    return lax.rsqrt(_half_sum(sq, upper) * (1.0 / HEAD_DIM) + EPS)


def _half_rscale(sq, upper):
    return lax.rsqrt(_half_sum(sq, upper) + HEAD_DIM * EPS) * LOG2E


def _head_norm(x, gain):
    lower = lax.broadcasted_iota(jnp.int32, (1, LANES), 1) < HEAD_DIM
    cols = []
    for c in range(x.shape[1] // LANES):
        lanes = slice(c * LANES, (c + 1) * LANES)
        xc = x[:, lanes]
        sq = xc * xc
        r = jnp.where(lower, _half_rms_factor(sq, upper=False), _half_rms_factor(sq, upper=True))
        cols.append(xc * r * gain[:, lanes])
    return jnp.concatenate(cols, axis=1)


WOUT_ROWS_A = (
    [(g * POOL_GROUP_DIM, POOL_GROUP_DIM, g * GROUP_LANES) for g in range(POOL_GROUPS)]
    + [(MAIN_WIDTH + g * HEAD_DIM, HEAD_DIM, (g + 1) * GROUP_LANES - HEAD_DIM) for g in range(MEM_HEADS)])
WOUT_ROWS_B = (
    [((kvh * SWA_GROUP + g) * HEAD_DIM, HEAD_DIM, g * GROUP_LANES + kvh * HEAD_DIM)
     for kvh in range(SWA_KV_HEADS) for g in range(SWA_GROUP)]
    + [(MAIN_WIDTH, MEM_WIDTH, MAIN_WIDTH)])


def _permute_rows_bf16(dst_ref, src_ref, chunks):
    for src0, n, dst0 in chunks:
        dst_ref[dst0:dst0 + n] = src_ref[src0:src0 + n].astype(BF16)


def _softmax_rows_log2(s2):
    m = jnp.max(s2, axis=-1, keepdims=True)
    e = jnp.exp2(s2 - m)
    return e * (1.0 / jnp.sum(e, axis=-1, keepdims=True))


def _rows_only(a, r0, r1, dst0=None, n=None):
    n = a.shape[0] if n is None else n
    dst0 = r0 if dst0 is None else dst0
    pieces = []
    if dst0 > 0:
        pieces.append(jnp.zeros((dst0, a.shape[1]), a.dtype))
    pieces.append(a[r0:r1])
    rest = n - dst0 - (r1 - r0)
    if rest > 0:
        pieces.append(jnp.zeros((rest, a.shape[1]), a.dtype))
    return jnp.concatenate(pieces, axis=0)


def _mem_kv_kernel(mem_ref, g_ref, w_ref, gk_ref, mkt_ref, mvw_ref):
    m = mem_ref[0]
    lane_slot = lax.broadcasted_iota(jnp.int32, (1, MEM_WIDTH), 1) // HEAD_DIM
    last = MEM_HEADS - 1
    for l in range(DEPTH):
        xn = _rms(m, g_ref[l]).astype(BF16)
        kv = _dot(xn, w_ref[l].astype(BF16))
        mkt_ref[l, 0] = _head_norm(kv[:, :MEM_WIDTH], gk_ref[l]).T.astype(BF16)
        v = kv[:, MEM_WIDTH:]
        for hm in range(MEM_HEADS):
            if l < N_A:
                moved = v if hm == last else pltpu.roll(v, (last - hm) * HEAD_DIM, 1)
                blk = jnp.where(lane_slot == last, moved, 0.0)
            else:
                blk = jnp.where(lane_slot == hm, v, 0.0)
            mvw_ref[l, 0, :, hm * GROUP_LANES:(hm + 1) * GROUP_LANES] = blk.astype(BF16)


def _mem_kv(mem, mem_norm, w_mem_kv, gk_mem):
    B = mem.shape[0]
    const3 = lambda b: (0, 0, 0)
    return pl.pallas_call(
        _mem_kv_kernel,
        grid=(B,),
        in_specs=[
            pl.BlockSpec((1, MEM_LEN, D_MODEL), lambda b: (b, 0, 0)),
            pl.BlockSpec((DEPTH, 1, D_MODEL), const3),
            pl.BlockSpec((DEPTH, D_MODEL, 2 * MEM_WIDTH), const3, pipeline_mode=pl.Buffered(1)),
            pl.BlockSpec((DEPTH, 1, MEM_WIDTH), const3),
        ],
        out_specs=[
            pl.BlockSpec((DEPTH, 1, MEM_WIDTH, MEM_LEN), lambda b: (0, b, 0, 0)),
            pl.BlockSpec((DEPTH, 1, MEM_LEN, D_MODEL), lambda b: (0, b, 0, 0)),
        ],
        out_shape=[
            jax.ShapeDtypeStruct((DEPTH, B, MEM_WIDTH, MEM_LEN), BF16),
            jax.ShapeDtypeStruct((DEPTH, B, MEM_LEN, D_MODEL), BF16),
        ],
        compiler_params=pltpu.CompilerParams(
            dimension_semantics=("arbitrary",), vmem_limit_bytes=VMEM_LIMIT),
        name="mem_kv",
    )(mem, mem_norm, w_mem_kv, gk_mem)


def _mix_a_kernel(h_ref, g_ref, win_ref, gq_ref, poolw_ref, pscale_ref, mkt_ref, mvw_ref, wout_ref,
                  wupf_ref, wdnf_ref, o_ref, wupb_ref, wdnb_ref, halo_ref, ext_ref, cat_ref, woutp_ref, xn_ref):
    i = pl.program_id(1)
    tm = h_ref.shape[1]
    wupb_ref[...] = wupf_ref[...].astype(BF16)
    wdnb_ref[...] = wdnf_ref[...].astype(BF16)

    @pl.when(jnp.logical_and(pl.program_id(0) == 0, i == 0))
    def _():
        _permute_rows_bf16(woutp_ref, wout_ref, WOUT_ROWS_A)

    @pl.when(i == 0)
    def _():
        halo_ref[...] = jnp.zeros_like(halo_ref)

    mkt = mkt_ref[0, 0]
    t1 = (lax.broadcasted_iota(jnp.int32, (POOL_HALO, GROUP_LANES), 0) + (i * tm + 1)).astype(F32)
    nsub = tm // SUB_MIX

    def win_piece(sb, c):
        srows = slice(sb * SUB_MIX, (sb + 1) * SUB_MIX)
        lanes = slice(c * GROUP_LANES, (c + 1) * GROUP_LANES)
        if c == 0:
            xn_ref[srows] = _rms(h_ref[0, srows], g_ref[...]).astype(BF16)
        proj = _dot(xn_ref[srows], win_ref[:, lanes])
        if sb == 0:
            ext_ref[0:POOL_HALO, lanes] = halo_ref[:, lanes]
        ext_ref[POOL_HALO + sb * SUB_MIX:POOL_HALO + (sb + 1) * SUB_MIX, lanes] = proj
        if sb == nsub - 1:
            halo_ref[:, lanes] = proj[SUB_MIX - POOL_HALO:]

    def wout_piece(sb, c):
        srows = slice(sb * SUB_MIX, (sb + 1) * SUB_MIX)
        lanes = slice(c * GROUP_LANES, (c + 1) * GROUP_LANES)
        o_ref[0, srows, lanes] = h_ref[0, srows, lanes] + _dot(cat_ref[srows], woutp_ref[:, lanes])

    def group_block(sb, g):
        srows = slice(sb * SUB_MIX, (sb + 1) * SUB_MIX)
        erows = slice(POOL_HALO + sb * SUB_MIX, POOL_HALO + (sb + 1) * SUB_MIX)
        lanes = slice(g * GROUP_LANES, (g + 1) * GROUP_LANES)
        s = ext_ref[sb * SUB_MIX:POOL_HALO + (sb + 1) * SUB_MIX, lanes]
        for k in range(g + 1):
            s = s + pltpu.roll(s, 1 << k, 0)
        s = s[POOL_HALO:]
        win = float(POOL_WINDOWS[g])
        if sb == 0:
            pooled = jnp.concatenate(
                [s[:POOL_HALO] / jnp.minimum(t1, win), s[POOL_HALO:] * (1.0 / win)], axis=0)
        else:
            pooled = s * (1.0 / win)
        d = (pooled - ext_ref[erows, lanes]).astype(BF16)
        mixed = _dot(d, poolw_ref[g]) * pscale_ref[:, lanes]

        mlanes = slice((g + 1) * GROUP_LANES - LANES, (g + 1) * GROUP_LANES)
        qc = ext_ref[erows, mlanes]
        r = _half_rscale(qc * qc, upper=True)
        qn = (qc * gq_ref[:, mlanes]).astype(BF16)
        ktm = _rows_only(mkt, g * HEAD_DIM, (g + 1) * HEAD_DIM, dst0=LANES - HEAD_DIM, n=LANES)
        p = _softmax_rows_log2(_dot(qn, ktm) * r).astype(BF16)
        memo = _dot(p, mvw_ref[0, 0, :, lanes])
        cat_ref[srows, lanes] = (mixed + memo).astype(BF16)

    for c in range(N_GROUPS):
        win_piece(0, c)
    for sb in range(nsub):
        fill = []
        if sb + 1 < nsub:
            fill += [(win_piece, sb + 1, c) for c in range(N_GROUPS)]
        if sb >= 1:
            fill += [(wout_piece, sb - 1, c) for c in range(N_GROUPS)]
        per = -(-len(fill) // N_GROUPS)
        for g in range(N_GROUPS):
            for f in fill[g * per:(g + 1) * per]:
                f[0](*f[1:])
            group_block(sb, g)
    for c in range(N_GROUPS):
        wout_piece(nsub - 1, c)


def _mlp_cast_specs(layer, n_steps, step_of):
    up_rows, dn_rows = D_MODEL // n_steps, D_FF // n_steps
    in_specs = [
        pl.BlockSpec((None, up_rows, D_FF), lambda b, i: (layer, step_of(b, i), 0)),
        pl.BlockSpec((None, dn_rows, D_MODEL), lambda b, i: (layer, step_of(b, i), 0)),
    ]
    out_specs = [
        pl.BlockSpec((up_rows, D_FF), lambda b, i: (step_of(b, i), 0)),
        pl.BlockSpec((dn_rows, D_MODEL), lambda b, i: (step_of(b, i), 0)),
    ]
    out_shape = [jax.ShapeDtypeStruct((D_MODEL, D_FF), BF16), jax.ShapeDtypeStruct((D_FF, D_MODEL), BF16)]
    return in_specs, out_specs, out_shape


def _mix_a(layer, h, g, w_in, gq, poolw, pscale, mkt, mvw, w_out, w_up, w_down):
    B, S, _ = h.shape
    tm = TM_MIX
    nt = S // tm
    const2 = lambda b, i: (0, 0)
    cast_in, cast_out, cast_shape = _mlp_cast_specs(layer, B * nt, lambda b, i: b * nt + i)
    return pl.pallas_call(
        _mix_a_kernel,
        grid=(B, S // tm),
        in_specs=[
            pl.BlockSpec((1, tm, D_MODEL), lambda b, i: (b, i, 0)),
            pl.BlockSpec((1, D_MODEL), const2),
            pl.BlockSpec((D_MODEL, D_MODEL), const2),
            pl.BlockSpec((1, D_MODEL), const2),
            pl.BlockSpec((N_GROUPS, GROUP_LANES, GROUP_LANES), lambda b, i: (0, 0, 0)),
            pl.BlockSpec((1, D_MODEL), const2),
            pl.BlockSpec((1, 1, MEM_WIDTH, MEM_LEN), lambda b, i: (layer, b, 0, 0)),
            pl.BlockSpec((1, 1, MEM_LEN, D_MODEL), lambda b, i: (layer, b, 0, 0)),
            pl.BlockSpec((None, D_MODEL, D_MODEL), lambda b, i: (layer, 0, 0), pipeline_mode=pl.Buffered(1)),
        ] + cast_in,
        out_specs=[pl.BlockSpec((1, tm, D_MODEL), lambda b, i: (b, i, 0))] + cast_out,
        out_shape=[jax.ShapeDtypeStruct(h.shape, F32)] + cast_shape,
        scratch_shapes=[
            pltpu.VMEM((POOL_HALO, D_MODEL), F32),
            pltpu.VMEM((tm + POOL_HALO, D_MODEL), F32),
            pltpu.VMEM((tm, D_MODEL), BF16),
            pltpu.VMEM((D_MODEL, D_MODEL), BF16),
            pltpu.VMEM((tm, D_MODEL), BF16),
        ],
        compiler_params=pltpu.CompilerParams(
            dimension_semantics=("arbitrary", "arbitrary"), vmem_limit_bytes=VMEM_LIMIT),
        name=f"mix_pool_{layer}",
    )(h, g, w_in, gq, poolw, pscale, mkt, mvw, w_out, w_up, w_down)


def _mix_b_kernel(sinks_ref, h_ref, g_ref, win_ref, gq_ref, ktc_ref, ktp_ref, vc_ref, vp_ref,
                  bias_ref, mkt_ref, mvw_ref, wout_ref, wupf_ref, wdnf_ref, o_ref, wupb_ref, wdnb_ref,
                  xn_ref, proj_ref, cat_ref, woutp_ref):
    i = pl.program_id(1)
    tm = h_ref.shape[1]
    nq = tm // BLOCK
    wupb_ref[...] = wupf_ref[...].astype(BF16)
    wdnb_ref[...] = wdnf_ref[...].astype(BF16)

    @pl.when(jnp.logical_and(pl.program_id(0) == 0, i == 0))
    def _():
        _permute_rows_bf16(woutp_ref, wout_ref, WOUT_ROWS_B)

    tri = (lax.broadcasted_iota(jnp.int32, (BLOCK, BLOCK), 0)
           >= lax.broadcasted_iota(jnp.int32, (BLOCK, BLOCK), 1))
    pen = jnp.where(tri, 0.0, jnp.where(i == 0, NEG, 0.0))
    lower_b = jnp.where(tri, 1.0, 0.0).astype(BF16)
    upper_b = jnp.where(tri, 0.0, 1.0).astype(BF16)

    lane_head = lax.broadcasted_iota(jnp.int32, (BLOCK, KV_HALF), 1) // HEAD_DIM
    vblocks = [vp_ref[...]] + [vc_ref[b * BLOCK:(b + 1) * BLOCK] for b in range(nq)]
    vmask = [[jnp.where(lane_head == kvh, vb, jnp.zeros_like(vb)) for kvh in range(SWA_KV_HEADS)]
             for vb in vblocks]

    mkt = mkt_ref[0, 0]
    mvst = jnp.concatenate(
        [mvw_ref[0, 0, :, hm * GROUP_LANES:(hm + 1) * GROUP_LANES] for hm in range(MEM_HEADS)], axis=0)
    mem_lanes = slice(SWA_GROUP * GROUP_LANES, D_MODEL)

    def win_piece(sb, c):
        srows = slice(sb * SUB_MIX, (sb + 1) * SUB_MIX)
        lanes = slice(c * GROUP_LANES, (c + 1) * GROUP_LANES)
        if c == 0:
            xn_ref[srows] = _rms(h_ref[0, srows], g_ref[...]).astype(BF16)
        proj_ref[srows, lanes] = _dot(xn_ref[srows], win_ref[:, lanes])

    def wout_piece(sb, c):
        srows = slice(sb * SUB_MIX, (sb + 1) * SUB_MIX)
        lanes = slice(c * GROUP_LANES, (c + 1) * GROUP_LANES)
        o_ref[0, srows, lanes] = h_ref[0, srows, lanes] + _dot(cat_ref[srows], woutp_ref[:, lanes])

    def swa_block(qb):
        r0 = qb * BLOCK
        rows = slice(r0, r0 + BLOCK)
        if qb == 0:
            ktw = jnp.concatenate([ktp_ref[...], ktc_ref[:, 0:BLOCK]], axis=1)
        else:
            ktw = ktc_ref[:, r0 - BLOCK:r0 + BLOCK]
        pblk = proj_ref[rows, 0:MAIN_WIDTH]
        rcol = []
        for c in range(MAIN_WIDTH // LANES):
            pc = pblk[:, c * LANES:(c + 1) * LANES]
            sq = pc * pc
            rcol.append(_half_rscale(sq, upper=False))
            rcol.append(_half_rscale(sq, upper=True))
        qn = (pblk * gq_ref[:, 0:MAIN_WIDTH]).astype(BF16)
        qst = jnp.concatenate(
            [qn[:, g * GROUP_LANES:(g + 1) * GROUP_LANES] for g in range(SWA_GROUP)], axis=0)
        p_all = []
        v_all = []
        for kvh in range(SWA_KV_HEADS):
            ktm = _rows_only(ktw, kvh * HEAD_DIM, (kvh + 1) * HEAD_DIM)
            s_all = _dot(qst, ktm)
            ps = []
            for g in range(SWA_GROUP):
                hq = kvh * SWA_GROUP + g
                sg = s_all[g * BLOCK:(g + 1) * BLOCK]
                s = (jnp.where(tri, sg[:, BLOCK:], sg[:, :BLOCK]) * rcol[g * SWA_KV_HEADS + kvh]
                     + bias_ref[hq])
                if qb == 0:
                    s = s + pen
                sink = sinks_ref[hq] * LOG2E
                m = jnp.max(s, axis=-1, keepdims=True)
                e = jnp.exp2(s - m)
                den = jnp.sum(e, axis=-1, keepdims=True) + jnp.exp2(sink - m)
                p = (e * (1.0 / den)).astype(BF16)
                ps.append(jnp.concatenate([p * upper_b, p * lower_b], axis=1))
            p_all.append(jnp.concatenate(ps, axis=0))
            v_all.append(jnp.concatenate([vmask[qb][kvh], vmask[qb + 1][kvh]], axis=0))
        o = _dot(jnp.concatenate(p_all, axis=1), jnp.concatenate(v_all, axis=0))
        for g in range(SWA_GROUP):
            cat_ref[rows, g * GROUP_LANES:(g + 1) * GROUP_LANES] = o[g * BLOCK:(g + 1) * BLOCK].astype(BF16)

    def mem_block(sb):
        srows = slice(sb * SUB_MIX, (sb + 1) * SUB_MIX)
        pmem = proj_ref[srows, mem_lanes]
        qm = (pmem * gq_ref[:, mem_lanes]).astype(BF16)
        pm = []
        for hm in range(MEM_HEADS):
            pc = pmem[:, (hm // 2) * LANES:(hm // 2 + 1) * LANES]
            r = _half_rscale(pc * pc, upper=bool(hm % 2))
            ktm = _rows_only(mkt, hm * HEAD_DIM, (hm + 1) * HEAD_DIM)
            pm.append(_softmax_rows_log2(_dot(qm, ktm) * r).astype(BF16))
        cat_ref[srows, mem_lanes] = _dot(jnp.concatenate(pm, axis=1), mvst).astype(BF16)

    nsub = tm // SUB_MIX
    qps = SUB_MIX // BLOCK
    for c in range(N_GROUPS):
        win_piece(0, c)
    for sb in range(nsub):
        main = [(swa_block, qb) for qb in range(sb * qps, (sb + 1) * qps)] + [(mem_block, sb)]
        fill = []
        if sb + 1 < nsub:
            fill += [(win_piece, sb + 1, c) for c in range(N_GROUPS)]
        if sb >= 1:
            fill += [(wout_piece, sb - 1, c) for c in range(N_GROUPS)]
        per = -(-len(fill) // len(main))
        for k, item in enumerate(main):
            for f in fill[k * per:(k + 1) * per]:
                f[0](*f[1:])
            item[0](*item[1:])
    for c in range(N_GROUPS):
        wout_piece(nsub - 1, c)


def _mix_b(layer, h, sinks, g, w_in, gq, kt, v, bias, mkt, mvw, w_out, w_up, w_down):
    B, S, _ = h.shape
    tm = TM_MIX
    nq = tm // BLOCK
    nt = S // tm
    const2 = lambda b, i: (0, 0)
    cast_in, cast_out, cast_shape = _mlp_cast_specs(layer, B * nt, lambda b, i: b * nt + i)
    return pl.pallas_call(
        _mix_b_kernel,
        grid=(B, S // tm),
        in_specs=[
            pl.BlockSpec(memory_space=pltpu.SMEM),
            pl.BlockSpec((1, tm, D_MODEL), lambda b, i: (b, i, 0)),
            pl.BlockSpec((1, D_MODEL), const2),
            pl.BlockSpec((D_MODEL, D_MODEL), const2),
            pl.BlockSpec((1, D_MODEL), const2),
            pl.BlockSpec((KV_HALF, tm), lambda b, i: (0, b * nt + i)),
            pl.BlockSpec((KV_HALF, BLOCK), lambda b, i: (0, b * nt * nq + jnp.maximum(i * nq - 1, 0))),
            pl.BlockSpec((tm, KV_HALF), lambda b, i: (b * nt + i, 0)),
            pl.BlockSpec((BLOCK, KV_HALF), lambda b, i: (b * nt * nq + jnp.maximum(i * nq - 1, 0), 0)),
            pl.BlockSpec((SWA_Q_HEADS, BLOCK, BLOCK), lambda b, i: (0, 0, 0)),
            pl.BlockSpec((1, 1, MEM_WIDTH, MEM_LEN), lambda b, i: (layer, b, 0, 0)),
            pl.BlockSpec((1, 1, MEM_LEN, D_MODEL), lambda b, i: (layer, b, 0, 0)),
            pl.BlockSpec((None, D_MODEL, D_MODEL), lambda b, i: (layer, 0, 0), pipeline_mode=pl.Buffered(1)),
        ] + cast_in,
        out_specs=[pl.BlockSpec((1, tm, D_MODEL), lambda b, i: (b, i, 0))] + cast_out,
        out_shape=[jax.ShapeDtypeStruct(h.shape, F32)] + cast_shape,
        scratch_shapes=[pltpu.VMEM((tm, D_MODEL), BF16), pltpu.VMEM((tm, D_MODEL), F32),
                        pltpu.VMEM((tm, D_MODEL), BF16), pltpu.VMEM((D_MODEL, D_MODEL), BF16)],
        compiler_params=pltpu.CompilerParams(
            dimension_semantics=("arbitrary", "arbitrary"), vmem_limit_bytes=VMEM_LIMIT),
        name=f"mix_swa_{layer}",
    )(sinks, h, g, w_in, gq, kt, kt, v, v, bias, mkt, mvw, w_out, w_up, w_down)


def _mlp_tile(h_ref, g_ref, wup_ref, wdn_ref):
    x = h_ref[...]
    xn = _rms(x, g_ref[...]).astype(BF16)
    u = jnp.maximum(_dot(xn, wup_ref[...]), 0.0)
    return x + _dot((u * u).astype(BF16), wdn_ref[...])


def _mlp_kernel(h_ref, g_ref, wup_ref, wdn_ref, o_ref):
    o_ref[...] = _mlp_tile(h_ref, g_ref, wup_ref, wdn_ref)


def _mlp_kv_kernel(h_ref, g_ref, wup_ref, wdn_ref, gkv_ref, wkv_ref, gk_ref, o_ref, kt_ref, v_ref):
    out = _mlp_tile(h_ref, g_ref, wup_ref, wdn_ref)
    o_ref[...] = out
    for r0 in range(0, out.shape[0], KV_CHUNK):
        rows = slice(r0, r0 + KV_CHUNK)
        kv = _dot(_rms(out[rows], gkv_ref[...]).astype(BF16), wkv_ref[...].astype(BF16))
        kn = _head_norm(kv[:, :KV_HALF], gk_ref[...])
        kt_ref[:, rows] = kn.T.astype(BF16)
        v_ref[rows] = kv[:, KV_HALF:].astype(BF16)


def _mlp(layer, h2d, g, w_up, w_down, kv_params=None):
    T = h2d.shape[0]
    tm = TM_MLP
    const = lambda i: (0, 0)
    in_specs = [
        pl.BlockSpec((tm, D_MODEL), lambda i: (i, 0)),
        pl.BlockSpec((1, D_MODEL), const),
        pl.BlockSpec((D_MODEL, D_FF), const, pipeline_mode=pl.Buffered(1)),
        pl.BlockSpec((D_FF, D_MODEL), const, pipeline_mode=pl.Buffered(1)),
    ]
    out_specs = pl.BlockSpec((tm, D_MODEL), lambda i: (i, 0))
    out_shape = jax.ShapeDtypeStruct(h2d.shape, F32)
    args = (h2d, g, w_up, w_down)
    body = _mlp_kernel
    if kv_params is not None:
        in_specs += [
            pl.BlockSpec((1, D_MODEL), const),
            pl.BlockSpec((D_MODEL, 2 * KV_HALF), const, pipeline_mode=pl.Buffered(1)),
            pl.BlockSpec((1, KV_HALF), const),
        ]
        out_specs = [out_specs, pl.BlockSpec((KV_HALF, tm), lambda i: (0, i)),
                     pl.BlockSpec((tm, KV_HALF), lambda i: (i, 0))]
        out_shape = [out_shape, jax.ShapeDtypeStruct((KV_HALF, T), BF16),
                     jax.ShapeDtypeStruct((T, KV_HALF), BF16)]
        args += tuple(kv_params)
        body = _mlp_kv_kernel
    return pl.pallas_call(
        body,
        grid=(T // tm,),
        in_specs=in_specs,
        out_specs=out_specs,
        out_shape=out_shape,
        compiler_params=pltpu.CompilerParams(
            dimension_semantics=("arbitrary",), vmem_limit_bytes=VMEM_LIMIT),
        name=f"mlp_{layer}",
    )(*args)


def _cols_a(main, memq):
    lead = main.shape[:-1]
    m = main.reshape(*lead, POOL_GROUPS, POOL_GROUP_DIM)
    q = memq.reshape(*lead, MEM_HEADS, HEAD_DIM)
    return jnp.concatenate([m, q], axis=-1).reshape(*lead, D_MODEL)


def _cols_b(main, memq):
    lead = main.shape[:-1]
    m = main.reshape(*lead, SWA_KV_HEADS, SWA_GROUP, HEAD_DIM)
    m = jnp.swapaxes(m, -3, -2).reshape(*lead, MAIN_WIDTH)
    return jnp.concatenate([m, memq], axis=-1)


def _perm_in(w_in_l, cols):
    return cols(w_in_l[:, :MAIN_WIDTH], w_in_l[:, MAIN_WIDTH:]).astype(BF16)


def _swa_bias():
    qi = jnp.arange(BLOCK, dtype=jnp.int32)[:, None]
    kj = jnp.arange(BLOCK, dtype=jnp.int32)[None, :]
    dist = jnp.where(kj <= qi, qi - kj, qi - kj + BLOCK)
    slopes = jnp.exp2(-8.0 * jnp.arange(1, SWA_Q_HEADS + 1, dtype=F32) / SWA_Q_HEADS)
    return -(slopes[:, None, None] * dist.astype(F32)[None]) * LOG2E


def kernel(x, mem, norm_mix, w_in, pool_w, pool_scale, kv_norm, w_kv, k_norm, q_norm, sinks, mem_norm,
           w_mem_kv, mem_q_norm, mem_k_norm, w_out, norm_mlp, w_up, w_down):
    B, S, _ = x.shape
    zeros_main = jnp.zeros((MAIN_WIDTH,), F32)
    zeros_mem = jnp.zeros((MEM_WIDTH,), F32)

    gk_mem = jnp.tile(mem_k_norm, (1, MEM_HEADS)).reshape(DEPTH, 1, MEM_WIDTH)
    mkt, mvw = _mem_kv(mem, mem_norm.reshape(DEPTH, 1, D_MODEL), w_mem_kv, gk_mem)

    bias = _swa_bias()

    h = x
    kt = v = None
    for l in range(DEPTH):
        g_mix = norm_mix[l].reshape(1, D_MODEL)
        if l < N_A:
            w_in_p = _perm_in(w_in[l], _cols_a)
            gq = _cols_a(zeros_main, jnp.tile(mem_q_norm[l], MEM_HEADS)).reshape(1, D_MODEL)
            poolw = jnp.pad(pool_w[l], ((0, 0), (0, GROUP_LANES - POOL_GROUP_DIM),
                                        (0, GROUP_LANES - POOL_GROUP_DIM))).astype(BF16)
            pscale = _cols_a(pool_scale[l], zeros_mem).reshape(1, D_MODEL)
            h, w_up_b, w_down_b = _mix_a(l, h, g_mix, w_in_p, gq, poolw, pscale, mkt, mvw, w_out, w_up, w_down)
        else:
            j = l - N_A
            w_in_p = _perm_in(w_in[l], _cols_b)
            gq = _cols_b(jnp.tile(q_norm[j], SWA_Q_HEADS), jnp.tile(mem_q_norm[l], MEM_HEADS))
            h, w_up_b, w_down_b = _mix_b(l, h, sinks[j], g_mix, w_in_p, gq.reshape(1, D_MODEL), kt, v, bias,
                                         mkt, mvw, w_out, w_up, w_down)
        mlp_args = (l, h.reshape(B * S, D_MODEL), norm_mlp[l].reshape(1, D_MODEL), w_up_b, w_down_b)
        if l == N_A - 1:
            gk = jnp.tile(k_norm, SWA_KV_HEADS).reshape(1, KV_HALF)
            h2d, kt, v = _mlp(*mlp_args, kv_params=(kv_norm.reshape(1, D_MODEL), w_kv, gk))
        else:
            h2d = _mlp(*mlp_args)
        h = h2d.reshape(B, S, D_MODEL)
    return h
```

```python
import jax
import jax.numpy as jnp
from jax import lax
from jax.experimental import pallas as pl
from jax.experimental.pallas import tpu as pltpu

F32 = jnp.float32
BF16 = jnp.bfloat16

D_MODEL = 1024
DEPTH = 4
N_A = DEPTH // 2
HEAD_DIM = 64
MEM_LEN = 256
MEM_HEADS = 4
MEM_WIDTH = MEM_HEADS * HEAD_DIM
MAIN_WIDTH = D_MODEL - MEM_WIDTH
POOL_WINDOWS = (2, 4, 8, 16)
POOL_GROUPS = len(POOL_WINDOWS)
POOL_GROUP_DIM = MAIN_WIDTH // POOL_GROUPS
SWA_Q_HEADS = MAIN_WIDTH // HEAD_DIM
SWA_KV_HEADS = 4
SWA_GROUP = SWA_Q_HEADS // SWA_KV_HEADS
KV_HALF = SWA_KV_HEADS * HEAD_DIM
BLOCK = 128
D_FF = 4 * D_MODEL
EPS = 1e-6
SCALE = HEAD_DIM ** -0.5
NEG = -1e30
LOG2E = 1.4426950408889634

LANES = 128
GROUP_LANES = 256
N_GROUPS = D_MODEL // GROUP_LANES
POOL_HALO = 16

TM_MIX = 1024
SUB_MIX = 512
TM_MLP = 1024
KV_CHUNK = 256
VMEM_LIMIT = 52 * 1024 * 1024

assert abs(SCALE * SCALE * HEAD_DIM - 1.0) < 1e-12


def _dot(a, b):
    return jnp.dot(a, b, preferred_element_type=F32)


def _rms(x, g):
    ms = jnp.mean(x * x, axis=-1, keepdims=True)
    return x * lax.rsqrt(ms + EPS) * g


def _half_sum(sq, upper):
    lane = lax.broadcasted_iota(jnp.int32, (1, LANES), 1)
    keep = (lane >= HEAD_DIM) if upper else (lane < HEAD_DIM)
    return jnp.sum(jnp.where(keep, sq, 0.0), axis=-1, keepdims=True)


def _half_rms_factor(sq, upper):
    return lax.rsqrt(_half_sum(sq, upper) * (1.0 / HEAD_DIM) + EPS)


def _half_rscale(sq, upper):
    return lax.rsqrt(_half_sum(sq, upper) + HEAD_DIM * EPS) * LOG2E


def _head_norm(x, gain):
    lower = lax.broadcasted_iota(jnp.int32, (1, LANES), 1) < HEAD_DIM
    cols = []
    for c in range(x.shape[1] // LANES):
        lanes = slice(c * LANES, (c + 1) * LANES)
        xc = x[:, lanes]
        sq = xc * xc
        r = jnp.where(lower, _half_rms_factor(sq, upper=False), _half_rms_factor(sq, upper=True))
        cols.append(xc * r * gain[:, lanes])
    return jnp.concatenate(cols, axis=1)


WOUT_ROWS_A = (
    [(g * POOL_GROUP_DIM, POOL_GROUP_DIM, g * GROUP_LANES) for g in range(POOL_GROUPS)]
    + [(MAIN_WIDTH + g * HEAD_DIM, HEAD_DIM, (g + 1) * GROUP_LANES - HEAD_DIM) for g in range(MEM_HEADS)])
WOUT_ROWS_B = (
    [((kvh * SWA_GROUP + g) * HEAD_DIM, HEAD_DIM, g * GROUP_LANES + kvh * HEAD_DIM)
     for kvh in range(SWA_KV_HEADS) for g in range(SWA_GROUP)]
    + [(MAIN_WIDTH, MEM_WIDTH, MAIN_WIDTH)])


def _permute_rows_bf16(dst_ref, src_ref, chunks):
    for src0, n, dst0 in chunks:
        dst_ref[dst0:dst0 + n] = src_ref[src0:src0 + n].astype(BF16)


def _softmax_rows_log2(s2):
    m = jnp.max(s2, axis=-1, keepdims=True)
    e = jnp.exp2(s2 - m)
    return e * (1.0 / jnp.sum(e, axis=-1, keepdims=True))


def _rows_only(a, r0, r1, dst0=None, n=None):
    n = a.shape[0] if n is None else n
    dst0 = r0 if dst0 is None else dst0
    pieces = []
    if dst0 > 0:
        pieces.append(jnp.zeros((dst0, a.shape[1]), a.dtype))
    pieces.append(a[r0:r1])
    rest = n - dst0 - (r1 - r0)
    if rest > 0:
        pieces.append(jnp.zeros((rest, a.shape[1]), a.dtype))
    return jnp.concatenate(pieces, axis=0)


def _mem_kv_into(mem_ref, g_ref, w_ref, gk_ref, mkt_ref, mvw_ref, pool_layer):
    lane_slot = lax.broadcasted_iota(jnp.int32, (1, MEM_WIDTH), 1) // HEAD_DIM
    last = MEM_HEADS - 1
    xn = _rms(mem_ref[0], g_ref[...]).astype(BF16)
    kv = _dot(xn, w_ref[...].astype(BF16))
    mkt_ref[0, 0] = _head_norm(kv[:, :MEM_WIDTH], gk_ref[...]).T.astype(BF16)
    v = kv[:, MEM_WIDTH:]
    for hm in range(MEM_HEADS):
        if pool_layer:
            moved = v if hm == last else pltpu.roll(v, (last - hm) * HEAD_DIM, 1)
            blk = jnp.where(lane_slot == last, moved, 0.0)
        else:
            blk = jnp.where(lane_slot == hm, v, 0.0)
        mvw_ref[0, 0, :, hm * GROUP_LANES:(hm + 1) * GROUP_LANES] = blk.astype(BF16)


def _mem_specs(layer):
    return [
        pl.BlockSpec((1, MEM_LEN, D_MODEL), lambda b, i: (b, 0, 0)),
        pl.BlockSpec((None, 1, D_MODEL), lambda b, i: (layer, 0, 0)),
        pl.BlockSpec((None, D_MODEL, 2 * MEM_WIDTH), lambda b, i: (layer, 0, 0), pipeline_mode=pl.Buffered(1)),
        pl.BlockSpec((None, 1, MEM_WIDTH), lambda b, i: (layer, 0, 0)),
    ]


MEM_SCRATCH = [pltpu.VMEM((1, 1, MEM_WIDTH, MEM_LEN), BF16), pltpu.VMEM((1, 1, MEM_LEN, D_MODEL), BF16)]


def _mix_a_kernel(h_ref, g_ref, win_ref, gq_ref, poolw_ref, pscale_ref, mem_ref, gmem_ref, wmem_ref, gkm_ref,
                  wout_ref, wupf_ref, wdnf_ref, o_ref, wupb_ref, wdnb_ref, halo_ref, ext_ref, cat_ref, woutp_ref,
                  xn_ref, mkt_ref, mvw_ref):
    i = pl.program_id(1)
    tm = h_ref.shape[1]
    wupb_ref[...] = wupf_ref[...].astype(BF16)
    wdnb_ref[...] = wdnf_ref[...].astype(BF16)

    @pl.when(jnp.logical_and(pl.program_id(0) == 0, i == 0))
    def _():
        _permute_rows_bf16(woutp_ref, wout_ref, WOUT_ROWS_A)

    @pl.when(i == 0)
    def _():
        halo_ref[...] = jnp.zeros_like(halo_ref)
        _mem_kv_into(mem_ref, gmem_ref, wmem_ref, gkm_ref, mkt_ref, mvw_ref, pool_layer=True)

    mkt = mkt_ref[0, 0]
    t1 = (lax.broadcasted_iota(jnp.int32, (POOL_HALO, GROUP_LANES), 0) + (i * tm + 1)).astype(F32)
    nsub = tm // SUB_MIX

    def win_piece(sb, c):
        srows = slice(sb * SUB_MIX, (sb + 1) * SUB_MIX)
        lanes = slice(c * GROUP_LANES, (c + 1) * GROUP_LANES)
        if c == 0:
            xn_ref[srows] = _rms(h_ref[0, srows], g_ref[...]).astype(BF16)
        proj = _dot(xn_ref[srows], win_ref[:, lanes])
        if sb == 0:
            ext_ref[0:POOL_HALO, lanes] = halo_ref[:, lanes]
        ext_ref[POOL_HALO + sb * SUB_MIX:POOL_HALO + (sb + 1) * SUB_MIX, lanes] = proj
        if sb == nsub - 1:
            halo_ref[:, lanes] = proj[SUB_MIX - POOL_HALO:]

    def wout_piece(sb, c):
        srows = slice(sb * SUB_MIX, (sb + 1) * SUB_MIX)
        lanes = slice(c * GROUP_LANES, (c + 1) * GROUP_LANES)
        o_ref[0, srows, lanes] = h_ref[0, srows, lanes] + _dot(cat_ref[srows], woutp_ref[:, lanes])

    def group_block(sb, g):
        srows = slice(sb * SUB_MIX, (sb + 1) * SUB_MIX)
        erows = slice(POOL_HALO + sb * SUB_MIX, POOL_HALO + (sb + 1) * SUB_MIX)
        lanes = slice(g * GROUP_LANES, (g + 1) * GROUP_LANES)
        s = ext_ref[sb * SUB_MIX:POOL_HALO + (sb + 1) * SUB_MIX, lanes]
        for k in range(g + 1):
            s = s + pltpu.roll(s, 1 << k, 0)
        s = s[POOL_HALO:]
        win = float(POOL_WINDOWS[g])
        if sb == 0:
            pooled = jnp.concatenate(
                [s[:POOL_HALO] / jnp.minimum(t1, win), s[POOL_HALO:] * (1.0 / win)], axis=0)
        else:
            pooled = s * (1.0 / win)
        d = (pooled - ext_ref[erows, lanes]).astype(BF16)
        mixed = _dot(d, poolw_ref[g]) * pscale_ref[:, lanes]

        mlanes = slice((g + 1) * GROUP_LANES - LANES, (g + 1) * GROUP_LANES)
        qc = ext_ref[erows, mlanes]
        r = _half_rscale(qc * qc, upper=True)
        qn = (qc * gq_ref[:, mlanes]).astype(BF16)
        ktm = _rows_only(mkt, g * HEAD_DIM, (g + 1) * HEAD_DIM, dst0=LANES - HEAD_DIM, n=LANES)
        p = _softmax_rows_log2(_dot(qn, ktm) * r).astype(BF16)
        memo = _dot(p, mvw_ref[0, 0, :, lanes])
        cat_ref[srows, lanes] = (mixed + memo).astype(BF16)

    for c in range(N_GROUPS):
        win_piece(0, c)
    for sb in range(nsub):
        fill = []
        if sb + 1 < nsub:
            fill += [(win_piece, sb + 1, c) for c in range(N_GROUPS)]
        if sb >= 1:
            fill += [(wout_piece, sb - 1, c) for c in range(N_GROUPS)]
        per = -(-len(fill) // N_GROUPS)
        for g in range(N_GROUPS):
            for f in fill[g * per:(g + 1) * per]:
                f[0](*f[1:])
            group_block(sb, g)
    for c in range(N_GROUPS):
        wout_piece(nsub - 1, c)


def _mlp_cast_specs(layer, n_steps, step_of):
    up_rows, dn_rows = D_MODEL // n_steps, D_FF // n_steps
    in_specs = [
        pl.BlockSpec((None, up_rows, D_FF), lambda b, i: (layer, step_of(b, i), 0)),
        pl.BlockSpec((None, dn_rows, D_MODEL), lambda b, i: (layer, step_of(b, i), 0)),
    ]
    out_specs = [
        pl.BlockSpec((up_rows, D_FF), lambda b, i: (step_of(b, i), 0)),
        pl.BlockSpec((dn_rows, D_MODEL), lambda b, i: (step_of(b, i), 0)),
    ]
    out_shape = [jax.ShapeDtypeStruct((D_MODEL, D_FF), BF16), jax.ShapeDtypeStruct((D_FF, D_MODEL), BF16)]
    return in_specs, out_specs, out_shape


def _mix_a(layer, h, g, w_in, gq, poolw, pscale, mem_args, w_out, w_up, w_down):
    B, S, _ = h.shape
    tm = TM_MIX
    nt = S // tm
    const2 = lambda b, i: (0, 0)
    cast_in, cast_out, cast_shape = _mlp_cast_specs(layer, B * nt, lambda b, i: b * nt + i)
    return pl.pallas_call(
        _mix_a_kernel,
        grid=(B, S // tm),
        in_specs=[
            pl.BlockSpec((1, tm, D_MODEL), lambda b, i: (b, i, 0)),
            pl.BlockSpec((1, D_MODEL), const2),
            pl.BlockSpec((D_MODEL, D_MODEL), const2),
            pl.BlockSpec((1, D_MODEL), const2),
            pl.BlockSpec((N_GROUPS, GROUP_LANES, GROUP_LANES), lambda b, i: (0, 0, 0)),
            pl.BlockSpec((1, D_MODEL), const2),
        ] + _mem_specs(layer) + [
            pl.BlockSpec((None, D_MODEL, D_MODEL), lambda b, i: (layer, 0, 0), pipeline_mode=pl.Buffered(1)),
        ] + cast_in,
        out_specs=[pl.BlockSpec((1, tm, D_MODEL), lambda b, i: (b, i, 0))] + cast_out,
        out_shape=[jax.ShapeDtypeStruct(h.shape, F32)] + cast_shape,
        scratch_shapes=[
            pltpu.VMEM((POOL_HALO, D_MODEL), F32),
            pltpu.VMEM((tm + POOL_HALO, D_MODEL), F32),
            pltpu.VMEM((tm, D_MODEL), BF16),
            pltpu.VMEM((D_MODEL, D_MODEL), BF16),
            pltpu.VMEM((tm, D_MODEL), BF16),
        ] + MEM_SCRATCH,
        compiler_params=pltpu.CompilerParams(
            dimension_semantics=("arbitrary", "arbitrary"), vmem_limit_bytes=VMEM_LIMIT),
        name=f"mix_pool_{layer}",
    )(h, g, w_in, gq, poolw, pscale, *mem_args, w_out, w_up, w_down)


def _mix_b_kernel(sinks_ref, h_ref, g_ref, win_ref, gq_ref, ktc_ref, ktp_ref, vc_ref, vp_ref,
                  bias_ref, mem_ref, gmem_ref, wmem_ref, gkm_ref, wout_ref, wupf_ref, wdnf_ref,
                  o_ref, wupb_ref, wdnb_ref, xn_ref, proj_ref, cat_ref, woutp_ref, mkt_ref, mvw_ref):
    i = pl.program_id(1)
    tm = h_ref.shape[1]
    nq = tm // BLOCK
    wupb_ref[...] = wupf_ref[...].astype(BF16)
    wdnb_ref[...] = wdnf_ref[...].astype(BF16)

    @pl.when(jnp.logical_and(pl.program_id(0) == 0, i == 0))
    def _():
        _permute_rows_bf16(woutp_ref, wout_ref, WOUT_ROWS_B)

    @pl.when(i == 0)
    def _():
        _mem_kv_into(mem_ref, gmem_ref, wmem_ref, gkm_ref, mkt_ref, mvw_ref, pool_layer=False)

    tri = (lax.broadcasted_iota(jnp.int32, (BLOCK, BLOCK), 0)
           >= lax.broadcasted_iota(jnp.int32, (BLOCK, BLOCK), 1))
    pen = jnp.where(tri, 0.0, jnp.where(i == 0, NEG, 0.0))
    lower_b = jnp.where(tri, 1.0, 0.0).astype(BF16)
    upper_b = jnp.where(tri, 0.0, 1.0).astype(BF16)

    lane_head = lax.broadcasted_iota(jnp.int32, (BLOCK, KV_HALF), 1) // HEAD_DIM
    vblocks = [vp_ref[...]] + [vc_ref[b * BLOCK:(b + 1) * BLOCK] for b in range(nq)]
    vmask = [[jnp.where(lane_head == kvh, vb, jnp.zeros_like(vb)) for kvh in range(SWA_KV_HEADS)]
             for vb in vblocks]

    mkt = mkt_ref[0, 0]
    mvst = jnp.concatenate(
        [mvw_ref[0, 0, :, hm * GROUP_LANES:(hm + 1) * GROUP_LANES] for hm in range(MEM_HEADS)], axis=0)
    mem_lanes = slice(SWA_GROUP * GROUP_LANES, D_MODEL)

    def win_piece(sb, c):
        srows = slice(sb * SUB_MIX, (sb + 1) * SUB_MIX)
        lanes = slice(c * GROUP_LANES, (c + 1) * GROUP_LANES)
        if c == 0:
            xn_ref[srows] = _rms(h_ref[0, srows], g_ref[...]).astype(BF16)
        proj_ref[srows, lanes] = _dot(xn_ref[srows], win_ref[:, lanes])

    def wout_piece(sb, c):
        srows = slice(sb * SUB_MIX, (sb + 1) * SUB_MIX)
        lanes = slice(c * GROUP_LANES, (c + 1) * GROUP_LANES)
        o_ref[0, srows, lanes] = h_ref[0, srows, lanes] + _dot(cat_ref[srows], woutp_ref[:, lanes])

    def swa_block(qb):
        r0 = qb * BLOCK
        rows = slice(r0, r0 + BLOCK)
        if qb == 0:
            ktw = jnp.concatenate([ktp_ref[...], ktc_ref[:, 0:BLOCK]], axis=1)
        else:
            ktw = ktc_ref[:, r0 - BLOCK:r0 + BLOCK]
        pblk = proj_ref[rows, 0:MAIN_WIDTH]
        rcol = []
        for c in range(MAIN_WIDTH // LANES):
            pc = pblk[:, c * LANES:(c + 1) * LANES]
            sq = pc * pc
            rcol.append(_half_rscale(sq, upper=False))
            rcol.append(_half_rscale(sq, upper=True))
        qn = (pblk * gq_ref[:, 0:MAIN_WIDTH]).astype(BF16)
        qst = jnp.concatenate(
            [qn[:, g * GROUP_LANES:(g + 1) * GROUP_LANES] for g in range(SWA_GROUP)], axis=0)
        p_all = []
        v_all = []
        for kvh in range(SWA_KV_HEADS):
            ktm = _rows_only(ktw, kvh * HEAD_DIM, (kvh + 1) * HEAD_DIM)
            s_all = _dot(qst, ktm)
            ps = []
            for g in range(SWA_GROUP):
                hq = kvh * SWA_GROUP + g
                sg = s_all[g * BLOCK:(g + 1) * BLOCK]
                s = (jnp.where(tri, sg[:, BLOCK:], sg[:, :BLOCK]) * rcol[g * SWA_KV_HEADS + kvh]
                     + bias_ref[hq])
                if qb == 0:
                    s = s + pen
                sink = sinks_ref[hq] * LOG2E
                m = jnp.max(s, axis=-1, keepdims=True)
                e = jnp.exp2(s - m)
                den = jnp.sum(e, axis=-1, keepdims=True) + jnp.exp2(sink - m)
                p = (e * (1.0 / den)).astype(BF16)
                ps.append(jnp.concatenate([p * upper_b, p * lower_b], axis=1))
            p_all.append(jnp.concatenate(ps, axis=0))
            v_all.append(jnp.concatenate([vmask[qb][kvh], vmask[qb + 1][kvh]], axis=0))
        o = _dot(jnp.concatenate(p_all, axis=1), jnp.concatenate(v_all, axis=0))
        for g in range(SWA_GROUP):
            cat_ref[rows, g * GROUP_LANES:(g + 1) * GROUP_LANES] = o[g * BLOCK:(g + 1) * BLOCK].astype(BF16)

    def mem_block(sb):
        srows = slice(sb * SUB_MIX, (sb + 1) * SUB_MIX)
        pmem = proj_ref[srows, mem_lanes]
        qm = (pmem * gq_ref[:, mem_lanes]).astype(BF16)
        pm = []
        for hm in range(MEM_HEADS):
            pc = pmem[:, (hm // 2) * LANES:(hm // 2 + 1) * LANES]
            r = _half_rscale(pc * pc, upper=bool(hm % 2))
            ktm = _rows_only(mkt, hm * HEAD_DIM, (hm + 1) * HEAD_DIM)
            pm.append(_softmax_rows_log2(_dot(qm, ktm) * r).astype(BF16))
        cat_ref[srows, mem_lanes] = _dot(jnp.concatenate(pm, axis=1), mvst).astype(BF16)

    nsub = tm // SUB_MIX
    qps = SUB_MIX // BLOCK
    for c in range(N_GROUPS):
        win_piece(0, c)
    for sb in range(nsub):
        main = [(swa_block, qb) for qb in range(sb * qps, (sb + 1) * qps)] + [(mem_block, sb)]
        fill = []
        if sb + 1 < nsub:
            fill += [(win_piece, sb + 1, c) for c in range(N_GROUPS)]
        if sb >= 1:
            fill += [(wout_piece, sb - 1, c) for c in range(N_GROUPS)]
        per = -(-len(fill) // len(main))
        for k, item in enumerate(main):
            for f in fill[k * per:(k + 1) * per]:
                f[0](*f[1:])
            item[0](*item[1:])
    for c in range(N_GROUPS):
        wout_piece(nsub - 1, c)


def _mix_b(layer, h, sinks, g, w_in, gq, kt, v, bias, mem_args, w_out, w_up, w_down):
    B, S, _ = h.shape
    tm = TM_MIX
    nq = tm // BLOCK
    nt = S // tm
    const2 = lambda b, i: (0, 0)
    cast_in, cast_out, cast_shape = _mlp_cast_specs(layer, B * nt, lambda b, i: b * nt + i)
    return pl.pallas_call(
        _mix_b_kernel,
        grid=(B, S // tm),
        in_specs=[
            pl.BlockSpec(memory_space=pltpu.SMEM),
            pl.BlockSpec((1, tm, D_MODEL), lambda b, i: (b, i, 0)),
            pl.BlockSpec((1, D_MODEL), const2),
            pl.BlockSpec((D_MODEL, D_MODEL), const2),
            pl.BlockSpec((1, D_MODEL), const2),
            pl.BlockSpec((KV_HALF, tm), lambda b, i: (0, b * nt + i)),
            pl.BlockSpec((KV_HALF, BLOCK), lambda b, i: (0, b * nt * nq + jnp.maximum(i * nq - 1, 0))),
            pl.BlockSpec((tm, KV_HALF), lambda b, i: (b * nt + i, 0)),
            pl.BlockSpec((BLOCK, KV_HALF), lambda b, i: (b * nt * nq + jnp.maximum(i * nq - 1, 0), 0)),
            pl.BlockSpec((SWA_Q_HEADS, BLOCK, BLOCK), lambda b, i: (0, 0, 0)),
        ] + _mem_specs(layer) + [
            pl.BlockSpec((None, D_MODEL, D_MODEL), lambda b, i: (layer, 0, 0), pipeline_mode=pl.Buffered(1)),
        ] + cast_in,
        out_specs=[pl.BlockSpec((1, tm, D_MODEL), lambda b, i: (b, i, 0))] + cast_out,
        out_shape=[jax.ShapeDtypeStruct(h.shape, F32)] + cast_shape,
        scratch_shapes=[pltpu.VMEM((tm, D_MODEL), BF16), pltpu.VMEM((tm, D_MODEL), F32),
                        pltpu.VMEM((tm, D_MODEL), BF16), pltpu.VMEM((D_MODEL, D_MODEL), BF16)] + MEM_SCRATCH,
        compiler_params=pltpu.CompilerParams(
            dimension_semantics=("arbitrary", "arbitrary"), vmem_limit_bytes=VMEM_LIMIT),
        name=f"mix_swa_{layer}",
    )(sinks, h, g, w_in, gq, kt, kt, v, v, bias, *mem_args, w_out, w_up, w_down)


def _mlp_tile(h_ref, g_ref, wup_ref, wdn_ref):
    x = h_ref[...]
    xn = _rms(x, g_ref[...]).astype(BF16)
    u = jnp.maximum(_dot(xn, wup_ref[...]), 0.0)
    return x + _dot((u * u).astype(BF16), wdn_ref[...])


def _mlp_kernel(h_ref, g_ref, wup_ref, wdn_ref, o_ref):
    o_ref[...] = _mlp_tile(h_ref, g_ref, wup_ref, wdn_ref)


def _mlp_kv_kernel(h_ref, g_ref, wup_ref, wdn_ref, gkv_ref, wkv_ref, gk_ref, o_ref, kt_ref, v_ref):
    out = _mlp_tile(h_ref, g_ref, wup_ref, wdn_ref)
    o_ref[...] = out
    for r0 in range(0, out.shape[0], KV_CHUNK):
        rows = slice(r0, r0 + KV_CHUNK)
        kv = _dot(_rms(out[rows], gkv_ref[...]).astype(BF16), wkv_ref[...].astype(BF16))
        kn = _head_norm(kv[:, :KV_HALF], gk_ref[...])
        kt_ref[:, rows] = kn.T.astype(BF16)
        v_ref[rows] = kv[:, KV_HALF:].astype(BF16)


def _mlp(layer, h2d, g, w_up, w_down, kv_params=None):
    T = h2d.shape[0]
    tm = TM_MLP
    const = lambda i: (0, 0)
    in_specs = [
        pl.BlockSpec((tm, D_MODEL), lambda i: (i, 0)),
        pl.BlockSpec((1, D_MODEL), const),
        pl.BlockSpec((D_MODEL, D_FF), const, pipeline_mode=pl.Buffered(1)),
        pl.BlockSpec((D_FF, D_MODEL), const, pipeline_mode=pl.Buffered(1)),
    ]
    out_specs = pl.BlockSpec((tm, D_MODEL), lambda i: (i, 0))
    out_shape = jax.ShapeDtypeStruct(h2d.shape, F32)
    args = (h2d, g, w_up, w_down)
    body = _mlp_kernel
    if kv_params is not None:
        in_specs += [
            pl.BlockSpec((1, D_MODEL), const),
            pl.BlockSpec((D_MODEL, 2 * KV_HALF), const, pipeline_mode=pl.Buffered(1)),
            pl.BlockSpec((1, KV_HALF), const),
        ]
        out_specs = [out_specs, pl.BlockSpec((KV_HALF, tm), lambda i: (0, i)),
                     pl.BlockSpec((tm, KV_HALF), lambda i: (i, 0))]
        out_shape = [out_shape, jax.ShapeDtypeStruct((KV_HALF, T), BF16),
                     jax.ShapeDtypeStruct((T, KV_HALF), BF16)]
        args += tuple(kv_params)
        body = _mlp_kv_kernel
    return pl.pallas_call(
        body,
        grid=(T // tm,),
        in_specs=in_specs,
        out_specs=out_specs,
        out_shape=out_shape,
        compiler_params=pltpu.CompilerParams(
            dimension_semantics=("arbitrary",), vmem_limit_bytes=VMEM_LIMIT),
        name=f"mlp_{layer}",
    )(*args)


def _cols_a(main, memq):
    lead = main.shape[:-1]
    m = main.reshape(*lead, POOL_GROUPS, POOL_GROUP_DIM)
    q = memq.reshape(*lead, MEM_HEADS, HEAD_DIM)
    return jnp.concatenate([m, q], axis=-1).reshape(*lead, D_MODEL)


def _cols_b(main, memq):
    lead = main.shape[:-1]
    m = main.reshape(*lead, SWA_KV_HEADS, SWA_GROUP, HEAD_DIM)
    m = jnp.swapaxes(m, -3, -2).reshape(*lead, MAIN_WIDTH)
    return jnp.concatenate([m, memq], axis=-1)


def _perm_in(w_in_l, cols):
    return cols(w_in_l[:, :MAIN_WIDTH], w_in_l[:, MAIN_WIDTH:]).astype(BF16)


def _swa_bias():
    qi = jnp.arange(BLOCK, dtype=jnp.int32)[:, None]
    kj = jnp.arange(BLOCK, dtype=jnp.int32)[None, :]
    dist = jnp.where(kj <= qi, qi - kj, qi - kj + BLOCK)
    slopes = jnp.exp2(-8.0 * jnp.arange(1, SWA_Q_HEADS + 1, dtype=F32) / SWA_Q_HEADS)
    return -(slopes[:, None, None] * dist.astype(F32)[None]) * LOG2E


def kernel(x, mem, norm_mix, w_in, pool_w, pool_scale, kv_norm, w_kv, k_norm, q_norm, sinks, mem_norm,
           w_mem_kv, mem_q_norm, mem_k_norm, w_out, norm_mlp, w_up, w_down):
    B, S, _ = x.shape
    zeros_main = jnp.zeros((MAIN_WIDTH,), F32)
    zeros_mem = jnp.zeros((MEM_WIDTH,), F32)

    gk_mem = jnp.tile(mem_k_norm, (1, MEM_HEADS)).reshape(DEPTH, 1, MEM_WIDTH)
    mem_args = (mem, mem_norm.reshape(DEPTH, 1, D_MODEL), w_mem_kv, gk_mem)

    bias = _swa_bias()

    h = x
    kt = v = None
    for l in range(DEPTH):
        g_mix = norm_mix[l].reshape(1, D_MODEL)
        if l < N_A:
            w_in_p = _perm_in(w_in[l], _cols_a)
            gq = _cols_a(zeros_main, jnp.tile(mem_q_norm[l], MEM_HEADS)).reshape(1, D_MODEL)
            poolw = jnp.pad(pool_w[l], ((0, 0), (0, GROUP_LANES - POOL_GROUP_DIM),
                                        (0, GROUP_LANES - POOL_GROUP_DIM))).astype(BF16)
            pscale = _cols_a(pool_scale[l], zeros_mem).reshape(1, D_MODEL)
            h, w_up_b, w_down_b = _mix_a(l, h, g_mix, w_in_p, gq, poolw, pscale, mem_args, w_out, w_up, w_down)
        else:
            j = l - N_A
            w_in_p = _perm_in(w_in[l], _cols_b)
            gq = _cols_b(jnp.tile(q_norm[j], SWA_Q_HEADS), jnp.tile(mem_q_norm[l], MEM_HEADS))
            h, w_up_b, w_down_b = _mix_b(l, h, sinks[j], g_mix, w_in_p, gq.reshape(1, D_MODEL), kt, v, bias,
                                         mem_args, w_out, w_up, w_down)
        mlp_args = (l, h.reshape(B * S, D_MODEL), norm_mlp[l].reshape(1, D_MODEL), w_up_b, w_down_b)
        if l == N_A - 1:
            gk = jnp.tile(k_norm, SWA_KV_HEADS).reshape(1, KV_HALF)
            h2d, kt, v = _mlp(*mlp_args, kv_params=(kv_norm.reshape(1, D_MODEL), w_kv, gk))
        else:
            h2d = _mlp(*mlp_args)
        h = h2d.reshape(B, S, D_MODEL)
    return h
```
